```python
import math
import jax, jax.numpy as jnp
from jax import lax
import numpy as np


D_MODEL = 2048
BATCH = 2
SEQ = 8192
DEPTH = 4

F32 = jnp.float32
EPS = 1e-6

HG_HEADS = 6
HG_DK = 128
HG_DV = 128
HG_WIDTH = HG_HEADS * HG_DV
HG_CHUNK = 64

RET_HEADS = 6
RET_DK = 64
RET_DV = 128
RET_WIDTH = RET_HEADS * RET_DV
RET_CHUNK = 128
ROPE_BASE = 10000.0

DIL_SLOTS = 4
DIL_HD = 128
DIL_GROUPS = ((128, 1), (512, 4), (2048, 16))
DIL_WIDTH = DIL_SLOTS * DIL_HD
DIL_HEADS = DIL_SLOTS * len(DIL_GROUPS)

MIX_WIDTH = HG_WIDTH + RET_WIDTH + DIL_WIDTH
D_FF = 4 * D_MODEL
REL_BUCKETS = 32
REL_MAX_DIST = 1024

IN_SPLITS = (HG_HEADS * HG_DK, HG_WIDTH, HG_HEADS * HG_DK, HG_HEADS * HG_DK, HG_WIDTH,
             RET_HEADS * RET_DK, RET_HEADS * RET_DK, RET_WIDTH, RET_WIDTH) + (DIL_WIDTH,) * (3 * len(DIL_GROUPS))
IN_WIDTH = sum(IN_SPLITS)

kernel_name = 'hybrid_hgrn2_retnet_dilated_encoder'


def rms_norm(x, g):
    xf = x.astype(F32)
    y = xf * lax.rsqrt(jnp.mean(xf * xf, axis=-1, keepdims=True) + EPS)
    return (y * g.astype(F32)).astype(x.dtype)


def head_rms(t, gain):
    return t * lax.rsqrt(jnp.mean(t * t, axis=-1, keepdims=True) + EPS) * gain.astype(F32)


def to_heads(t, n_heads):
    B, S, W = t.shape
    return t.astype(F32).reshape(B, S, n_heads, W // n_heads).transpose(0, 2, 1, 3)


def rope(t):
    S, d = t.shape[1], t.shape[-1]
    half = d // 2
    inv = ROPE_BASE ** (-jnp.arange(half, dtype=F32) / half)
    ang = jnp.arange(S, dtype=F32)[:, None] * inv[None, :]
    cos = jnp.cos(ang)[None, :, None, :]
    sin = jnp.sin(ang)[None, :, None, :]
    t1, t2 = t[..., :half], t[..., half:]
    return jnp.concatenate([t1 * cos - t2 * sin, t1 * sin + t2 * cos], axis=-1)


def hgrn2_chunk_scan(q, k, v, log_f):
    B, H, S, DK = q.shape
    DV = v.shape[-1]
    C = HG_CHUNK
    nc = S // C

    def chunks(t):
        return t.reshape(B, H, nc, C, t.shape[-1]).transpose(2, 0, 1, 3, 4)

    mask = jnp.tril(jnp.ones((C, C), dtype=bool))[:, :, None]

    def step(state, inp):
        qc, kc, vc, lc = inp
        b = jnp.cumsum(lc, axis=2)
        diff = b[:, :, :, None, :] - b[:, :, None, :, :]
        decay = jnp.exp(jnp.where(mask, diff, -jnp.inf))
        attn = jnp.einsum('bhik,bhjk,bhijk->bhij', qc, kc, decay)
        out = (jnp.einsum('bhij,bhjv->bhiv', attn, vc)
               + jnp.einsum('bhik,bhkv->bhiv', qc * jnp.exp(b), state))
        b_last = b[:, :, -1:, :]
        state = (jnp.exp(b_last)[:, :, 0, :, None] * state
                 + jnp.einsum('bhjk,bhjv->bhkv', kc * jnp.exp(b_last - b), vc))
        return state, out

    state0 = jnp.zeros((B, H, DK, DV), F32)
    _, out = lax.scan(step, state0, (chunks(q), chunks(k), chunks(v), chunks(log_f)))
    return out.transpose(1, 2, 0, 3, 4).reshape(B, H, S, DV)


def hgrn2_mixer(q, i_in, z_fwd, z_bwd, gate, lb_fwd, lb_bwd, norm_g):
    B, S, _ = q.shape
    qh = to_heads(q, HG_HEADS)
    vh = to_heads(i_in, HG_HEADS)

    def one_direction(z, lb, reverse):
        lbh = lb.astype(F32).reshape(HG_HEADS, 1, HG_DK)
        f = lbh + (1.0 - lbh) * jax.nn.sigmoid(to_heads(z, HG_HEADS))
        k = 1.0 - f
        lf = jnp.log(f)
        if reverse:
            fl = lambda t: jnp.flip(t, axis=2)
            return fl(hgrn2_chunk_scan(fl(qh), fl(k), fl(vh), fl(lf)))
        return hgrn2_chunk_scan(qh, k, vh, lf)

    o = one_direction(z_fwd, lb_fwd, False) + one_direction(z_bwd, lb_bwd, True)
    o = o.transpose(0, 2, 1, 3)
    o = o * lax.rsqrt(jnp.mean(o * o, axis=-1, keepdims=True) + EPS)
    o = o.reshape(B, S, HG_WIDTH) * norm_g.astype(F32)
    return o * jax.nn.silu(gate.astype(F32))


def retention_chunk(q, k, v, log_gamma):
    B, H, S, DK = q.shape
    DV = v.shape[-1]
    C = RET_CHUNK
    nc = S // C
    qc = q.reshape(B, H, nc, C, DK)
    kc = k.reshape(B, H, nc, C, DK)
    vc = v.reshape(B, H, nc, C, DV)
    idx = jnp.arange(C, dtype=F32)
    lg = log_gamma[:, None]
    rel = idx[:, None] - idx[None, :]
    decay = jnp.where(rel >= 0, jnp.exp(lg[:, :, None] * jnp.maximum(rel, 0.0)), 0.0)
    scores = jnp.einsum('bhnid,bhnjd->bhnij', qc, kc) * decay[None, :, None]
    intra = jnp.einsum('bhnij,bhnje->bhnie', scores, vc)
    zeta = jnp.exp(lg * (C - 1 - idx))
    xi = jnp.exp(lg * (idx + 1))
    kv = jnp.einsum('bhnjd,hj,bhnje->nbhde', kc, zeta, vc)
    chunk_decay = jnp.exp(log_gamma * C)[None, :, None, None]

    def step(state, kv_n):
        return chunk_decay * state + kv_n, state

    _, prev = lax.scan(step, jnp.zeros((B, H, DK, DV), F32), kv)
    cross = jnp.einsum('bhnid,hi,nbhde->bhnie', qc, xi, prev)
    return (intra + cross).reshape(B, H, S, DV)


def retention_mixer(q, k, v, gate, norm_g):
    B, S, _ = q.shape
    qh = rope(q.astype(F32).reshape(B, S, RET_HEADS, RET_DK)).transpose(0, 2, 1, 3)
    kh = rope(k.astype(F32).reshape(B, S, RET_HEADS, RET_DK)).transpose(0, 2, 1, 3) * RET_DK ** -0.5
    vh = to_heads(v, RET_HEADS)
    hidx = jnp.arange(RET_HEADS, dtype=F32)
    log_g_fwd = jnp.log1p(-jnp.exp2(-5.0 - hidx))
    log_g_bwd = log_g_fwd[::-1]
    fl = lambda t: jnp.flip(t, axis=2)
    o = retention_chunk(qh, kh, vh, log_g_fwd) + fl(retention_chunk(fl(qh), fl(kh), fl(vh), log_g_bwd))
    o = o.transpose(0, 2, 1, 3)
    mu = jnp.mean(o, axis=-1, keepdims=True)
    var = jnp.mean(jnp.square(o - mu), axis=-1, keepdims=True)
    o = ((o - mu) * lax.rsqrt(var + EPS)).reshape(B, S, RET_WIDTH) * norm_g.astype(F32)
    return o * jax.nn.silu(gate.astype(F32))


def t5_bucket(rel):
    nb = REL_BUCKETS // 2
    max_exact = nb // 2
    sign_off = jnp.where(rel > 0, nb, 0)
    n = jnp.abs(rel)
    nf = jnp.maximum(n, 1).astype(F32)
    large = max_exact + (jnp.log(nf / max_exact) / math.log(REL_MAX_DIST / max_exact)
                         * (nb - max_exact)).astype(jnp.int32)
    large = jnp.minimum(large, nb - 1)
    return sign_off + jnp.where(n < max_exact, n, large)


def dilated_local_attention(q, k, v, bias_table, dil, half):
    B, H, S, D = q.shape
    L = S // dil
    nb = -(-L // half)
    Lp = nb * half

    def to_res(t):
        return t.reshape(B, H, L, dil, D).transpose(0, 1, 3, 2, 4)

    qr, kr, vr = to_res(q), to_res(k), to_res(v)
    qb = jnp.pad(qr, ((0, 0), (0, 0), (0, 0), (0, Lp - L), (0, 0))).reshape(B, H, dil, nb, half, D)

    def band(t):
        tp = jnp.pad(t, ((0, 0), (0, 0), (0, 0), (half, Lp - L + half), (0, 0)))
        tp = tp.reshape(B, H, dil, nb + 2, half, D)
        return jnp.concatenate([tp[:, :, :, :-2], tp[:, :, :, 1:-1], tp[:, :, :, 2:]], axis=4)

    kb, vb = band(kr), band(vr)
    ii = jnp.arange(half)[:, None]
    jj = jnp.arange(3 * half)[None, :]
    rel = jj - half - ii
    bias = bias_table.astype(F32)[t5_bucket(rel * dil)].transpose(2, 0, 1)
    key_idx = jnp.arange(nb)[:, None, None] * half + jj[None] - half
    valid = (jnp.abs(rel) <= half)[None] & (key_idx >= 0) & (key_idx < L)
    s = jnp.einsum('bhrnqd,bhrnkd->bhrnqk', qb, kb) + bias[None, :, None, None]
    s = jnp.where(valid, s, -jnp.inf)
    m = jnp.max(s, axis=-1, keepdims=True)
    p = jnp.exp(s - m)
    den = jnp.sum(p, axis=-1)
    o = jnp.einsum('bhrnqk,bhrnkd->bhrnqd', p, vb) / den[..., None]
    lse = m[..., 0] + jnp.log(den)
    o = o.reshape(B, H, dil, Lp, D)[:, :, :, :L].transpose(0, 1, 3, 2, 4).reshape(B, H, S, D)
    lse = lse.reshape(B, H, dil, Lp)[..., :L].transpose(0, 1, 3, 2).reshape(B, H, S)
    return o, lse


def dilated_mixer(parts, rel_bias, q_gain, k_gain):
    B, S, _ = parts[0].shape
    outs, lses = [], []
    for g, (window, dil) in enumerate(DIL_GROUPS):
        half = window // (2 * dil)
        q = head_rms(parts[3 * g].astype(F32).reshape(B, S, DIL_SLOTS, DIL_HD), q_gain) * DIL_HD ** -0.5
        k = head_rms(parts[3 * g + 1].astype(F32).reshape(B, S, DIL_SLOTS, DIL_HD), k_gain)
        v = parts[3 * g + 2].astype(F32).reshape(B, S, DIL_SLOTS, DIL_HD)
        tbl = rel_bias[:, g * DIL_SLOTS:(g + 1) * DIL_SLOTS]
        o, lse = dilated_local_attention(q.transpose(0, 2, 1, 3), k.transpose(0, 2, 1, 3),
                                         v.transpose(0, 2, 1, 3), tbl, dil, half)
        outs.append(o)
        lses.append(lse)
    w = jax.nn.softmax(jnp.stack(lses, axis=0), axis=0)
    o = jnp.sum(w[..., None] * jnp.stack(outs, axis=0), axis=0)
    return o.transpose(0, 2, 1, 3).reshape(B, S, DIL_WIDTH)


def setup_inputs(seed: int = 0) -> dict:
    key = jax.random.key(seed)
    ks = jax.random.split(key, 14)

    def nrm(k, shape, scale):
        return jax.random.normal(k, shape, F32) * scale

    return {
        'x': nrm(ks[0], (BATCH, SEQ, D_MODEL), 1.0),
        'w_in': nrm(ks[1], (DEPTH, D_MODEL, IN_WIDTH), D_MODEL ** -0.5),
        'w_out': nrm(ks[2], (DEPTH, MIX_WIDTH, D_MODEL), MIX_WIDTH ** -0.5),
        'w_up': nrm(ks[3], (DEPTH, D_MODEL, D_FF), D_MODEL ** -0.5),
        'w_down': nrm(ks[4], (DEPTH, D_FF, D_MODEL), D_FF ** -0.5),
        'norm_mix': 1.0 + nrm(ks[5], (DEPTH, D_MODEL), 0.02),
        'norm_mlp': 1.0 + nrm(ks[6], (DEPTH, D_MODEL), 0.02),
        'hg_lb_fwd': nrm(ks[7], (DEPTH, HG_HEADS * HG_DK), 0.5),
        'hg_lb_bwd': nrm(ks[8], (DEPTH, HG_HEADS * HG_DK), 0.5),
        'hg_norm': 1.0 + nrm(ks[9], (DEPTH, HG_WIDTH), 0.02),
        'ret_norm': 1.0 + nrm(ks[10], (DEPTH, RET_WIDTH), 0.02),
        'q_norm': 1.0 + nrm(ks[11], (DEPTH, DIL_HD), 0.02),
        'k_norm': 1.0 + nrm(ks[12], (DEPTH, DIL_HD), 0.02),
        'rel_bias': nrm(ks[13], (REL_BUCKETS, DIL_HEADS), 0.1),
    }


def reference(x, w_in, w_out, w_up, w_down, norm_mix, norm_mlp, hg_lb_fwd, hg_lb_bwd,
              hg_norm, ret_norm, q_norm, k_norm, rel_bias):
    lb_fwd_all = jnp.cumsum(jax.nn.softmax(hg_lb_fwd.astype(F32), axis=0), axis=0)
    lb_bwd_all = jnp.cumsum(jax.nn.softmax(hg_lb_bwd.astype(F32), axis=0), axis=0)
    offsets = np.cumsum(IN_SPLITS)[:-1].tolist()
    for l in range(DEPTH):
        h = rms_norm(x, norm_mix[l]) @ w_in[l]
        p = jnp.split(h, offsets, axis=-1)
        y_a = hgrn2_mixer(p[0], p[1], p[2], p[3], p[4],
                          lb_fwd_all[l] - lb_fwd_all[0], lb_bwd_all[l] - lb_bwd_all[0], hg_norm[l])
        y_b = retention_mixer(p[5], p[6], p[7], p[8], ret_norm[l])
        y_c = dilated_mixer(p[9:], rel_bias, q_norm[l], k_norm[l])
        y = jnp.concatenate([y_a, y_b, y_c], axis=-1).astype(x.dtype)
        x = x + y @ w_out[l]
        hm = rms_norm(x, norm_mlp[l])
        x = x + jnp.square(jax.nn.relu(hm @ w_up[l])) @ w_down[l]
    return x
```

```python
import functools
import math

import numpy as np
import jax
import jax.numpy as jnp
from jax import lax
from jax.experimental import pallas as pl
from jax.experimental.pallas import tpu as pltpu

F32 = jnp.float32
BF16 = jnp.bfloat16
EPS = 1e-6

D_MODEL = 2048
DEPTH = 4
LANES = 128

HG_HEADS = 6
HG_W = HG_HEADS * LANES
RET_HEADS = 6
RET_DK = 64
RET_W = RET_HEADS * LANES
ROPE_BASE = 10000.0
DIL_SLOTS = 4
DIL_GROUPS = ((128, 1), (512, 4), (2048, 16))
DIL_W = DIL_SLOTS * LANES
DIL_HALF = 64
MIX_W = HG_W + RET_W + DIL_W
D_FF = 4 * D_MODEL
REL_BUCKETS = 32
REL_MAX_DIST = 1024

CB_HG_Q, CB_HG_V, CB_HG_ZF, CB_HG_ZB, CB_HG_G = 0, 6, 12, 18, 24
CB_RET_Q, CB_RET_K = 30, 33
CB_RET_V, CB_RET_G = 36, 42
CB_DIL = 48
IN_W = 10752
IN_CB = IN_W // LANES

H_DTYPE = F32
VMEM_LIMIT = 56 * 1024 * 1024

HG_CHUNK = 128
NEG_BIG = -1e30


def _cparams(sem):
    return pltpu.CompilerParams(dimension_semantics=sem, vmem_limit_bytes=VMEM_LIMIT)


def _dot(a, b):
    return jnp.dot(a, b, preferred_element_type=F32)


def _dot_nt(a, b):
    return lax.dot_general(a, b, (((1,), (1,)), ((), ())), preferred_element_type=F32)


def _dot_tn(a, b):
    return lax.dot_general(a, b, (((0,), (0,)), ((), ())), preferred_element_type=F32)


def _rms_rows(x, g):
    return x * lax.rsqrt(jnp.mean(x * x, axis=-1, keepdims=True) + EPS) * g


def _rmsnorm_kernel(x_ref, g_ref, o_ref):
    o_ref[...] = _rms_rows(x_ref[...], g_ref[...]).astype(o_ref.dtype)


def rmsnorm(x, g, tm):
    m, d = x.shape
    return pl.pallas_call(
        _rmsnorm_kernel,
        out_shape=jax.ShapeDtypeStruct((m, d), BF16),
        grid=(m // tm,),
        in_specs=[pl.BlockSpec((tm, d), lambda i: (i, 0)),
                  pl.BlockSpec((1, d), lambda i: (0, 0))],
        out_specs=pl.BlockSpec((tm, d), lambda i: (i, 0)),
        compiler_params=_cparams(("parallel",)),
        name="rmsnorm",
    )(x, g.reshape(1, d))


def _matmul_kernel(x_ref, w_ref, o_ref):
    o_ref[...] = _dot(x_ref[...], w_ref[...]).astype(o_ref.dtype)


def in_proj(xn, w, tm, tn):
    m, d = xn.shape
    n = w.shape[1]
    return pl.pallas_call(
        _matmul_kernel,
        out_shape=jax.ShapeDtypeStruct((m, n), H_DTYPE),
        grid=(m // tm, n // tn),
        in_specs=[pl.BlockSpec((tm, d), lambda i, j: (i, 0)),
                  pl.BlockSpec((d, tn), lambda i, j: (0, j))],
        out_specs=pl.BlockSpec((tm, tn), lambda i, j: (i, j)),
        compiler_params=_cparams(("parallel", "arbitrary")),
        name="in_proj",
    )(xn, w)


def _out_proj_kernel(ya_ref, yb_ref, yc_ref, w_ref, x_ref, g_ref, xo_ref, hn_ref):
    y = (_dot(ya_ref[...], w_ref[0:HG_W, :])
         + _dot(yb_ref[...], w_ref[HG_W:HG_W + RET_W, :])
         + _dot(yc_ref[...], w_ref[HG_W + RET_W:MIX_W, :]))
    xo = x_ref[...] + y
    xo_ref[...] = xo
    hn_ref[...] = _rms_rows(xo, g_ref[...]).astype(hn_ref.dtype)


def out_proj(ya, yb, yc, w, x, g, tm):
    m, d = x.shape
    row = lambda i: (i, 0)
    return pl.pallas_call(
        _out_proj_kernel,
        out_shape=(jax.ShapeDtypeStruct((m, d), F32), jax.ShapeDtypeStruct((m, d), BF16)),
        grid=(m // tm,),
        in_specs=[pl.BlockSpec((tm, HG_W), row), pl.BlockSpec((tm, RET_W), row),
                  pl.BlockSpec((tm, DIL_W), row),
                  pl.BlockSpec((MIX_W, d), lambda i: (0, 0)),
                  pl.BlockSpec((tm, d), row),
                  pl.BlockSpec((1, d), lambda i: (0, 0))],
        out_specs=(pl.BlockSpec((tm, d), row), pl.BlockSpec((tm, d), row)),
        compiler_params=_cparams(("parallel",)),
        name="out_proj",
    )(ya, yb, yc, w, x, g.reshape(1, d))


def _mlp_kernel(h_ref, wu_ref, wd_ref, x_ref, g_ref, xo_ref, *rest, nj, emit_norm):
    j = pl.program_id(1)

    @pl.when(j == 0)
    def _():
        xo_ref[...] = x_ref[...]

    a = jnp.maximum(_dot(h_ref[...], wu_ref[...]), 0.0)
    a = (a * a).astype(BF16)
    xo_ref[...] += _dot(a, wd_ref[...])

    if emit_norm:
        xn_ref = rest[0]

        @pl.when(j == nj - 1)
        def _():
            xn_ref[...] = _rms_rows(xo_ref[...], g_ref[...]).astype(xn_ref.dtype)


def mlp(hn, wu, wd, x, g, tm, tf, emit_norm):
    m, d = x.shape
    f = wu.shape[1]
    nj = f // tf
    row = lambda i, j: (i, 0)
    out_shape = [jax.ShapeDtypeStruct((m, d), F32)]
    out_specs = [pl.BlockSpec((tm, d), row)]
    if emit_norm:
        out_shape.append(jax.ShapeDtypeStruct((m, d), BF16))
        out_specs.append(pl.BlockSpec((tm, d), row))
    res = pl.pallas_call(
        functools.partial(_mlp_kernel, nj=nj, emit_norm=emit_norm),
        out_shape=tuple(out_shape),
        grid=(m // tm, nj),
        in_specs=[pl.BlockSpec((tm, d), row),
                  pl.BlockSpec((d, tf), lambda i, j: (0, j)),
                  pl.BlockSpec((tf, d), lambda i, j: (j, 0)),
                  pl.BlockSpec((tm, d), row),
                  pl.BlockSpec((1, d), lambda i, j: (0, 0))],
        out_specs=tuple(out_specs),
        compiler_params=_cparams(("parallel", "arbitrary")),
        name="mlp",
    )(hn, wu, wd, x, g.reshape(1, d))
    return res if emit_norm else (res[0], None)


def _hg_constants(c):
    i = np.arange(c)[:, None]
    j = np.arange(c)[None, :]
    x = i ^ j
    lvl = np.where(x == 0, 0, np.floor(np.log2(np.maximum(x, 1))).astype(np.int64) + 1)
    tri_f = (j <= i).astype(np.float32)
    tri_r = (j >= i).astype(np.float32)
    lv_f = np.where(i >= j, lvl, -1).astype(np.int32)
    lv_r = np.where(i <= j, lvl, -1).astype(np.int32)
    return (jnp.asarray(tri_f, BF16), jnp.asarray(tri_r, BF16),
            jnp.asarray(lv_f), jnp.asarray(lv_r))


def _hg_ref_rows(b, m, reverse, rowid):
    c = b.shape[0]
    g = 2 * m
    r = m if reverse else m - 1
    if g >= 8:
        b3 = b.reshape(c // g, g, LANES)
        return jnp.broadcast_to(b3[:, r:r + 1, :], b3.shape).reshape(c, LANES)
    pos = rowid % g
    out = b
    for p in range(g):
        off = r - p
        if off == 0:
            continue
        out = jnp.where(pos == p, pltpu.roll(b, (-off) % c, 0), out)
    return out


def _hg_chunk(q, k, v, lf, tri, lv, rowid, st, reverse):
    c = q.shape[0]
    hi = lf.astype(BF16)
    lo = (lf - hi.astype(F32)).astype(BF16)
    b = _dot(tri, hi) + _dot(tri, lo)
    vb = v.astype(BF16)
    a = jnp.where(lv == 0, _dot_nt(q.astype(BF16), k.astype(BF16)), 0.0)
    for ell in range(1, int(math.log2(c)) + 1):
        m = 2 ** (ell - 1)
        fac = jnp.exp(-jnp.abs(b - _hg_ref_rows(b, m, reverse, rowid)))
        s = _dot_nt((q * fac).astype(BF16), (k * fac).astype(BF16))
        a = jnp.where(lv == ell, s, a)
    out = _dot(a.astype(BF16), vb) + _dot_nt((q * jnp.exp(b)).astype(BF16), st.astype(BF16))
    b_edge = b[0:1, :] if reverse else b[c - 1:c, :]
    kk = (k * jnp.exp(b_edge - b)).astype(BF16)
    st_new = st * jnp.exp(b_edge) + _dot_tn(vb, kk)
    return out, st_new


def _hgrn2_kernel(q_ref, v_ref, zf_ref, zb_ref, g_ref, lbf_ref, lbb_ref, ng_ref,
                  trif_ref, trir_ref, lvf_ref, lvr_ref, o_ref, st_ref, ob_ref, *, t_rows, n_t):
    t = pl.program_id(2)
    c = HG_CHUNK
    n_c = t_rows // c
    rowid = lax.broadcasted_iota(jnp.int32, (c, LANES), 0)

    def sweep(z_ref, lb_ref, tri_ref, lv_ref, reverse, blk):
        lb = lb_ref[...]
        tri = tri_ref[...]
        lv = lv_ref[...]

        def body(ci, carry):
            cc = (n_c - 1 - ci) if reverse else ci
            rows = pl.ds(pl.multiple_of(cc * c, c), c)
            q = q_ref[rows, :].astype(F32)
            v = v_ref[rows, :].astype(F32)
            f = lb + (1.0 - lb) * jax.nn.sigmoid(z_ref[rows, :].astype(F32))
            out, st_new = _hg_chunk(q, 1.0 - f, v, jnp.log(f), tri, lv, rowid,
                                    st_ref[...], reverse)
            st_ref[...] = st_new
            srows = pl.ds(pl.multiple_of(blk * t_rows + cc * c, c), c)
            if reverse:
                ob_ref[srows, :] = out
            else:
                o = out + ob_ref[srows, :]
                gate = g_ref[rows, :].astype(F32)
                y = _rms_rows(o, ng_ref[...]) * (gate * jax.nn.sigmoid(gate))
                o_ref[rows, :] = y.astype(o_ref.dtype)
            return carry

        lax.fori_loop(0, n_c, body, 0)

    @pl.when((t == 0) | (t == n_t))
    def _():
        st_ref[...] = jnp.zeros_like(st_ref)

    @pl.when(t < n_t)
    def _():
        sweep(zb_ref, lbb_ref, trir_ref, lvr_ref, True, n_t - 1 - t)

    @pl.when(t >= n_t)
    def _():
        sweep(zf_ref, lbf_ref, trif_ref, lvf_ref, False, t - n_t)


def hgrn2_mixer(h3, lb_f, lb_b, norm_g, t_rows):
    bsz, s, _ = h3.shape
    n_t = s // t_rows
    tri_f, tri_r, lv_f, lv_r = _hg_constants(HG_CHUNK)

    def both(cb):
        return pl.BlockSpec((None, t_rows, LANES),
                            lambda b, h, t: (b, jnp.where(t < n_t, n_t - 1 - t, t - n_t), cb + h))

    def bwd_only(cb):
        return pl.BlockSpec((None, t_rows, LANES),
                            lambda b, h, t: (b, jnp.maximum(n_t - 1 - t, 0), cb + h))

    def fwd_only(cb):
        return pl.BlockSpec((None, t_rows, LANES),
                            lambda b, h, t: (b, jnp.maximum(t - n_t, 0), cb + h))

    head_row = pl.BlockSpec((None, 1, LANES), lambda b, h, t: (h, 0, 0))
    const = pl.BlockSpec((HG_CHUNK, HG_CHUNK), lambda b, h, t: (0, 0))
    return pl.pallas_call(
        functools.partial(_hgrn2_kernel, t_rows=t_rows, n_t=n_t),
        out_shape=jax.ShapeDtypeStruct((bsz, s, HG_W), BF16),
        grid=(bsz, HG_HEADS, 2 * n_t),
        in_specs=[both(CB_HG_Q), both(CB_HG_V), fwd_only(CB_HG_ZF), bwd_only(CB_HG_ZB),
                  fwd_only(CB_HG_G), head_row, head_row, head_row, const, const, const, const],
        out_specs=pl.BlockSpec((None, t_rows, LANES),
                               lambda b, h, t: (b, jnp.maximum(t - n_t, 0), h)),
        scratch_shapes=[pltpu.VMEM((LANES, LANES), F32), pltpu.VMEM((s, LANES), F32)],
        compiler_params=_cparams(("parallel", "parallel", "arbitrary")),
        name="hgrn2",
    )(h3, h3, h3, h3, h3, lb_f.reshape(HG_HEADS, 1, LANES), lb_b.reshape(HG_HEADS, 1, LANES),
      norm_g.reshape(HG_HEADS, 1, LANES), tri_f, tri_r, lv_f, lv_r)


def _rope_tables(s):
    half = RET_DK // 2
    inv = ROPE_BASE ** (-jnp.arange(half, dtype=F32) / half)
    ang = jnp.arange(s, dtype=F32)[:, None] * inv[None, :]
    cos, sin = jnp.cos(ang), jnp.sin(ang)
    cos_t = jnp.concatenate([cos, cos, cos, cos], axis=-1)
    sin_t = jnp.concatenate([-sin, sin, -sin, sin], axis=-1)
    return cos_t, sin_t


def _rope_pair(x, cos_t, sin_t, lane):
    half = RET_DK // 2
    swapped = jnp.where(lane % RET_DK < half, pltpu.roll(x, LANES - half, 1), pltpu.roll(x, half, 1))
    return x * cos_t + swapped * sin_t


def _retention_kernel(lg_ref, q_ref, k_ref, v_ref, g_ref, cos_ref, sin_ref, ng_ref, o_ref,
                      sb_ref, sbc_ref, sf_ref, d_ref, *, t_rows, n_t):
    pair = pl.program_id(1)
    t = pl.program_id(2)
    tr = t_rows
    lane = lax.broadcasted_iota(jnp.int32, (tr, LANES), 1)
    row = lax.broadcasted_iota(jnp.int32, (tr, LANES), 0).astype(F32)
    cos_t = cos_ref[...]
    sin_t = sin_ref[...]
    k = _rope_pair(k_ref[...].astype(F32), cos_t, sin_t, lane) * (RET_DK ** -0.5)

    @pl.when(t == 0)
    def _():
        sbc_ref[...] = jnp.zeros_like(sbc_ref)

    @pl.when(t == n_t)
    def _():
        sf_ref[...] = jnp.zeros_like(sf_ref)
        ri = lax.broadcasted_iota(jnp.int32, (tr, tr), 0)
        ci = lax.broadcasted_iota(jnp.int32, (tr, tr), 1)
        rel = (ri - ci).astype(F32)
        for hh in range(2):
            lg_f = lg_ref[0, 2 * pair + hh]
            lg_b = lg_ref[1, 2 * pair + hh]
            d_ref[hh] = (jnp.exp(lg_f * jnp.maximum(rel, 0.0) + lg_b * jnp.maximum(-rel, 0.0))
                         + jnp.where(rel == 0.0, 1.0, 0.0))

    @pl.when(t < n_t)
    def _():
        blk = n_t - 1 - t
        for hh in range(2):
            lg_b = lg_ref[1, 2 * pair + hh]
            km = jnp.where(lane // RET_DK == hh, k, 0.0)
            vb = v_ref[:, hh * LANES:(hh + 1) * LANES].astype(BF16)
            cur = sbc_ref[hh]
            sb_ref[hh, blk] = cur
            zeta = jnp.exp(lg_b * row)
            sbc_ref[hh] = cur * jnp.exp(lg_b * tr) + _dot_tn((km * zeta).astype(BF16), vb)

    @pl.when(t >= n_t)
    def _():
        blk = t - n_t
        q = _rope_pair(q_ref[...].astype(F32), cos_t, sin_t, lane)
        qb = q.astype(BF16)
        for hh in range(2):
            lg_f = lg_ref[0, 2 * pair + hh]
            lg_b = lg_ref[1, 2 * pair + hh]
            km = jnp.where(lane // RET_DK == hh, k, 0.0)
            vb = v_ref[:, hh * LANES:(hh + 1) * LANES].astype(BF16)
            p = (_dot_nt(qb, km.astype(BF16)) * d_ref[hh]).astype(BF16)
            sf = sf_ref[hh]
            o = (_dot(p, vb)
                 + _dot((q * jnp.exp(lg_f * (row + 1.0))).astype(BF16), sf.astype(BF16))
                 + _dot((q * jnp.exp(lg_b * (tr - row))).astype(BF16), sb_ref[hh, blk].astype(BF16)))
            zeta = jnp.exp(lg_f * (tr - 1.0 - row))
            sf_ref[hh] = sf * jnp.exp(lg_f * tr) + _dot_tn((km * zeta).astype(BF16), vb)
            mu = jnp.mean(o, axis=-1, keepdims=True)
            oc = o - mu
            var = jnp.mean(oc * oc, axis=-1, keepdims=True)
            gate = g_ref[:, hh * LANES:(hh + 1) * LANES].astype(F32)
            y = oc * lax.rsqrt(var + EPS) * ng_ref[:, hh * LANES:(hh + 1) * LANES]
            o_ref[:, hh * LANES:(hh + 1) * LANES] = (y * (gate * jax.nn.sigmoid(gate))).astype(o_ref.dtype)


def retention_mixer(h3, norm_g, t_rows):
    bsz, s, _ = h3.shape
    n_t = s // t_rows
    cos_t, sin_t = _rope_tables(s)
    hidx = jnp.arange(RET_HEADS, dtype=F32)
    lg_f = jnp.log1p(-jnp.exp2(-5.0 - hidx))
    lg = jnp.stack([lg_f, lg_f[::-1]], axis=0)
    blk_of = lambda t: jnp.where(t < n_t, n_t - 1 - t, t - n_t)
    fwd_blk = lambda t: jnp.maximum(t - n_t, 0)
    return pl.pallas_call(
        functools.partial(_retention_kernel, t_rows=t_rows, n_t=n_t),
        out_shape=jax.ShapeDtypeStruct((bsz, s, RET_W), BF16),
        grid=(bsz, RET_HEADS // 2, 2 * n_t),
        in_specs=[pl.BlockSpec(memory_space=pltpu.SMEM),
                  pl.BlockSpec((None, t_rows, LANES), lambda b, p, t: (b, fwd_blk(t), CB_RET_Q + p)),
                  pl.BlockSpec((None, t_rows, LANES), lambda b, p, t: (b, blk_of(t), CB_RET_K + p)),
                  pl.BlockSpec((None, t_rows, 2 * LANES), lambda b, p, t: (b, blk_of(t), CB_RET_V // 2 + p)),
                  pl.BlockSpec((None, t_rows, 2 * LANES), lambda b, p, t: (b, fwd_blk(t), CB_RET_G // 2 + p)),
                  pl.BlockSpec((t_rows, LANES), lambda b, p, t: (blk_of(t), 0)),
                  pl.BlockSpec((t_rows, LANES), lambda b, p, t: (blk_of(t), 0)),
                  pl.BlockSpec((1, 2 * LANES), lambda b, p, t: (0, p))],
        out_specs=pl.BlockSpec((None, t_rows, 2 * LANES), lambda b, p, t: (b, fwd_blk(t), p)),
        scratch_shapes=[pltpu.VMEM((2, n_t, LANES, LANES), F32),
                        pltpu.VMEM((2, LANES, LANES), F32),
                        pltpu.VMEM((2, LANES, LANES), F32),
                        pltpu.VMEM((2, t_rows, t_rows), F32)],
        compiler_params=_cparams(("parallel", "parallel", "arbitrary")),
        name="retention",
    )(lg, h3, h3, h3, h3, cos_t, sin_t, norm_g.reshape(1, RET_W))


ATT_QB = 128


def _t5_bucket(rel):
    nb = REL_BUCKETS // 2
    max_exact = nb // 2
    sign_off = jnp.where(rel > 0, nb, 0)
    n = jnp.abs(rel)
    nf = jnp.maximum(n, 1).astype(F32)
    large = max_exact + (jnp.log(nf / max_exact) / math.log(REL_MAX_DIST / max_exact)
                         * (nb - max_exact)).astype(jnp.int32)
    large = jnp.minimum(large, nb - 1)
    return sign_off + jnp.where(n < max_exact, n, large)


def _attn_kernel(tbl_ref, q_ref, kp_ref, k_ref, kn_ref, vp_ref, v_ref, vn_ref, qg_ref, kg_ref,
                 bkt_ref, o_ref, lse_ref, *, lq, l_total, head0):
    slot = pl.program_id(2)
    n = pl.program_id(3)
    kw = ATT_QB + 2 * DIL_HALF
    bkt = bkt_ref[...]
    bias = jnp.zeros((ATT_QB, kw), F32)
    for bucket in range(REL_BUCKETS):
        bias = jnp.where(bkt == bucket, tbl_ref[bucket, head0 + slot], bias)
    ii = lax.broadcasted_iota(jnp.int32, (ATT_QB, kw), 0)
    jj = lax.broadcasted_iota(jnp.int32, (ATT_QB, kw), 1)
    in_band = jnp.abs(jj - DIL_HALF - ii) <= DIL_HALF

    q = _rms_rows(q_ref[...].astype(F32), qg_ref[...]) * (LANES ** -0.5)
    kall = jnp.concatenate([kp_ref[...], k_ref[...], kn_ref[...]], axis=0).astype(F32)
    kall = _rms_rows(kall, kg_ref[...]).astype(BF16)
    vall = jnp.concatenate([vp_ref[...], v_ref[...], vn_ref[...]], axis=0).astype(BF16)
    qb16 = q.astype(BF16)
    for qb in range(lq // ATT_QB):
        r0 = qb * ATT_QB
        key_idx = n * lq + r0 - DIL_HALF + jj
        valid = in_band & (key_idx >= 0) & (key_idx < l_total)
        s = _dot_nt(qb16[r0:r0 + ATT_QB], kall[r0:r0 + kw]) + bias
        s = jnp.where(valid, s, NEG_BIG)
        m = jnp.max(s, axis=-1, keepdims=True)
        p = jnp.exp(s - m)
        den = jnp.sum(p, axis=-1, keepdims=True)
        o = _dot(p.astype(BF16), vall[r0:r0 + kw]) / den
        o_ref[r0:r0 + ATT_QB, :] = o
        lse_ref[r0:r0 + ATT_QB, :] = jnp.broadcast_to(m + jnp.log(den), (ATT_QB, LANES))


def dilated_group(h3, rel_bias, q_gain, k_gain, g, lq_max):
    bsz, s, _ = h3.shape
    dil = DIL_GROUPS[g][1]
    l_total = s // dil
    lq = min(l_total, lq_max)
    n_blk = l_total // lq
    hpb = lq // DIL_HALF
    n_halo = l_total // DIL_HALF
    hv = h3.reshape(bsz, l_total, dil * IN_W)
    cq = CB_DIL + 3 * g * DIL_SLOTS
    ck = cq + DIL_SLOTS
    cv = ck + DIL_SLOTS

    def main(cb):
        return pl.BlockSpec((None, lq, LANES), lambda b, r, sl, n: (b, n, r * IN_CB + cb + sl))

    def prev(cb):
        return pl.BlockSpec((None, DIL_HALF, LANES),
                            lambda b, r, sl, n: (b, jnp.maximum(n * hpb - 1, 0), r * IN_CB + cb + sl))

    def nxt(cb):
        return pl.BlockSpec((None, DIL_HALF, LANES),
                            lambda b, r, sl, n: (b, jnp.minimum((n + 1) * hpb, n_halo - 1), r * IN_CB + cb + sl))

    ii = jnp.arange(ATT_QB)[:, None]
    jj = jnp.arange(ATT_QB + 2 * DIL_HALF)[None, :]
    bkt = _t5_bucket((jj - DIL_HALF - ii) * dil).astype(jnp.int32)
    gain = pl.BlockSpec((1, LANES), lambda b, r, sl, n: (0, 0))
    out_spec = pl.BlockSpec((None, lq, LANES), lambda b, r, sl, n: (b, n, r * DIL_SLOTS + sl))
    o, lse = pl.pallas_call(
        functools.partial(_attn_kernel, lq=lq, l_total=l_total, head0=g * DIL_SLOTS),
        out_shape=(jax.ShapeDtypeStruct((bsz, l_total, dil * DIL_W), F32),
                   jax.ShapeDtypeStruct((bsz, l_total, dil * DIL_W), F32)),
        grid=(bsz, dil, DIL_SLOTS, n_blk),
        in_specs=[pl.BlockSpec(memory_space=pltpu.SMEM),
                  main(cq), prev(ck), main(ck), nxt(ck), prev(cv), main(cv), nxt(cv),
                  gain, gain,
                  pl.BlockSpec(bkt.shape, lambda b, r, sl, n: (0, 0))],
        out_specs=(out_spec, out_spec),
        compiler_params=_cparams(("parallel", "parallel", "parallel", "arbitrary")),
        name=f"dilated_attn_g{g}",
    )(rel_bias, hv, hv, hv, hv, hv, hv, hv, q_gain.reshape(1, LANES), k_gain.reshape(1, LANES), bkt)
    return o.reshape(bsz * s, DIL_W), lse.reshape(bsz * s, DIL_W)


def _combine_kernel(o0_ref, o1_ref, o2_ref, l0_ref, l1_ref, l2_ref, y_ref):
    l0, l1, l2 = l0_ref[...], l1_ref[...], l2_ref[...]
    m = jnp.maximum(jnp.maximum(l0, l1), l2)
    w0, w1, w2 = jnp.exp(l0 - m), jnp.exp(l1 - m), jnp.exp(l2 - m)
    y = (w0 * o0_ref[...] + w1 * o1_ref[...] + w2 * o2_ref[...]) / (w0 + w1 + w2)
    y_ref[...] = y.astype(y_ref.dtype)


def dilated_mixer(h3, rel_bias, q_gain, k_gain, lq_max, tm):
    outs = [dilated_group(h3, rel_bias, q_gain, k_gain, g, lq_max) for g in range(len(DIL_GROUPS))]
    m = outs[0][0].shape[0]
    spec = pl.BlockSpec((tm, DIL_W), lambda i: (i, 0))
    return pl.pallas_call(
        _combine_kernel,
        out_shape=jax.ShapeDtypeStruct((m, DIL_W), BF16),
        grid=(m // tm,),
        in_specs=[spec] * 6,
        out_specs=spec,
        compiler_params=_cparams(("parallel",)),
        name="dilated_combine",
    )(outs[0][0], outs[1][0], outs[2][0], outs[0][1], outs[1][1], outs[2][1])


def _tiles(bsz, s):
    m = bsz * s
    return dict(tm=min(1024, m), tm_small=min(512, m), hg_t=min(512, s), ret_t=min(512, s),
                lq_max=1024)


def kernel(x, w_in, w_out, w_up, w_down, norm_mix, norm_mlp, hg_lb_fwd, hg_lb_bwd,
           hg_norm, ret_norm, q_norm, k_norm, rel_bias):
    bsz, s, d = x.shape
    m = bsz * s
    tl = _tiles(bsz, s)
    lb_f = jnp.cumsum(jax.nn.softmax(hg_lb_fwd.astype(F32), axis=0), axis=0)
    lb_b = jnp.cumsum(jax.nn.softmax(hg_lb_bwd.astype(F32), axis=0), axis=0)
    lb_f = lb_f - lb_f[0]
    lb_b = lb_b - lb_b[0]
    w_in_b, w_out_b = w_in.astype(BF16), w_out.astype(BF16)
    w_up_b, w_down_b = w_up.astype(BF16), w_down.astype(BF16)

    xf = x.reshape(m, d)
    xn = rmsnorm(xf, norm_mix[0], tl["tm_small"])
    for l in range(DEPTH):
        h = in_proj(xn, w_in_b[l], tl["tm"], 1536)
        h3 = h.reshape(bsz, s, IN_W)
        ya = hgrn2_mixer(h3, lb_f[l], lb_b[l], hg_norm[l], tl["hg_t"]).reshape(m, HG_W)
        yb = retention_mixer(h3, ret_norm[l], tl["ret_t"]).reshape(m, RET_W)
        yc = dilated_mixer(h3, rel_bias, q_norm[l], k_norm[l], tl["lq_max"], tl["tm_small"])
        xf, hn = out_proj(ya, yb, yc, w_out_b[l], xf, norm_mlp[l], tl["tm_small"])
        last = l == DEPTH - 1
        xf, xn = mlp(hn, w_up_b[l], w_down_b[l], xf, norm_mix[(l + 1) % DEPTH],
                     tl["tm_small"], 1024, not last)
    return xf.reshape(bsz, s, d)
```

```python
import functools
import math

import numpy as np
import jax
import jax.numpy as jnp
from jax import lax
from jax.experimental import pallas as pl
from jax.experimental.pallas import tpu as pltpu

F32 = jnp.float32
BF16 = jnp.bfloat16
EPS = 1e-6

D_MODEL = 2048
DEPTH = 4
LANES = 128

HG_HEADS = 6
HG_W = HG_HEADS * LANES
RET_HEADS = 6
RET_DK = 64
RET_W = RET_HEADS * LANES
ROPE_BASE = 10000.0
DIL_SLOTS = 4
DIL_GROUPS = ((128, 1), (512, 4), (2048, 16))
DIL_W = DIL_SLOTS * LANES
DIL_HALF = 64
MIX_W = HG_W + RET_W + DIL_W
D_FF = 4 * D_MODEL
REL_BUCKETS = 32
REL_MAX_DIST = 1024

CB_HG_Q, CB_HG_V, CB_HG_ZF, CB_HG_ZB, CB_HG_G = 0, 6, 12, 18, 24
CB_RET_Q, CB_RET_K = 30, 33
CB_RET_V, CB_RET_G = 36, 42
CB_DIL = 48
IN_W = 10752
IN_CB = IN_W // LANES

H_DTYPE = F32
VMEM_LIMIT = 56 * 1024 * 1024

HG_CHUNK = 128
NEG_BIG = -1e30


def _cparams(sem):
    return pltpu.CompilerParams(dimension_semantics=sem, vmem_limit_bytes=VMEM_LIMIT)


def _dot(a, b):
    return jnp.dot(a, b, preferred_element_type=F32)


def _dot_nt(a, b):
    return lax.dot_general(a, b, (((1,), (1,)), ((), ())), preferred_element_type=F32)


def _dot_tn(a, b):
    return lax.dot_general(a, b, (((0,), (0,)), ((), ())), preferred_element_type=F32)


def _rms_rows(x, g):
    return x * lax.rsqrt(jnp.mean(x * x, axis=-1, keepdims=True) + EPS) * g


def _rmsnorm_kernel(x_ref, g_ref, o_ref):
    o_ref[...] = _rms_rows(x_ref[...], g_ref[...]).astype(o_ref.dtype)


def rmsnorm(x, g, tm):
    m, d = x.shape
    return pl.pallas_call(
        _rmsnorm_kernel,
        out_shape=jax.ShapeDtypeStruct((m, d), BF16),
        grid=(m // tm,),
        in_specs=[pl.BlockSpec((tm, d), lambda i: (i, 0)),
                  pl.BlockSpec((1, d), lambda i: (0, 0))],
        out_specs=pl.BlockSpec((tm, d), lambda i: (i, 0)),
        compiler_params=_cparams(("parallel",)),
        name="rmsnorm",
    )(x, g.reshape(1, d))


def _matmul_kernel(x_ref, w_ref, o_ref):
    o_ref[...] = _dot(x_ref[...], w_ref[...]).astype(o_ref.dtype)


def in_proj(xn, w_all, layer, tm, tn, col0, ncols):
    m, d = xn.shape
    j0 = col0 // tn
    return pl.pallas_call(
        _matmul_kernel,
        out_shape=jax.ShapeDtypeStruct((m, ncols), H_DTYPE),
        grid=(m // tm, ncols // tn),
        in_specs=[pl.BlockSpec((tm, d), lambda i, j: (i, 0)),
                  pl.BlockSpec((None, d, tn), lambda i, j: (layer, 0, j0 + j))],
        out_specs=pl.BlockSpec((tm, tn), lambda i, j: (i, j)),
        compiler_params=_cparams(("parallel", "arbitrary")),
        name="in_proj",
    )(xn, w_all)


def in_proj_groups(xn, w_all, layer, tm, col0):
    m, d = xn.shape
    tn = 3 * DIL_W
    j0 = col0 // tn
    return pl.pallas_call(
        _matmul_kernel,
        out_shape=jax.ShapeDtypeStruct((len(DIL_GROUPS), m, tn), H_DTYPE),
        grid=(m // tm, len(DIL_GROUPS)),
        in_specs=[pl.BlockSpec((tm, d), lambda i, j: (i, 0)),
                  pl.BlockSpec((None, d, tn), lambda i, j: (layer, 0, j0 + j))],
        out_specs=pl.BlockSpec((None, tm, tn), lambda i, j: (j, i, 0)),
        compiler_params=_cparams(("parallel", "arbitrary")),
        name="in_proj_groups",
    )(xn, w_all)


def _out_proj_kernel(ya_ref, yb_ref, yc_ref, w_ref, x_ref, g_ref, xo_ref, hn_ref):
    y = (_dot(ya_ref[...], w_ref[0:HG_W, :])
         + _dot(yb_ref[...], w_ref[HG_W:HG_W + RET_W, :])
         + _dot(yc_ref[...], w_ref[HG_W + RET_W:MIX_W, :]))
    xo = x_ref[...] + y
    xo_ref[...] = xo
    hn_ref[...] = _rms_rows(xo, g_ref[...]).astype(hn_ref.dtype)


def out_proj(ya, yb, yc, w_all, layer, x, g, tm):
    m, d = x.shape
    row = lambda i: (i, 0)
    return pl.pallas_call(
        _out_proj_kernel,
        out_shape=(jax.ShapeDtypeStruct((m, d), F32), jax.ShapeDtypeStruct((m, d), BF16)),
        grid=(m // tm,),
        in_specs=[pl.BlockSpec((tm, HG_W), row), pl.BlockSpec((tm, RET_W), row),
                  pl.BlockSpec((tm, DIL_W), row),
                  pl.BlockSpec((None, MIX_W, d), lambda i: (layer, 0, 0)),
                  pl.BlockSpec((tm, d), row),
                  pl.BlockSpec((1, d), lambda i: (0, 0))],
        out_specs=(pl.BlockSpec((tm, d), row), pl.BlockSpec((tm, d), row)),
        compiler_params=_cparams(("parallel",)),
        name="out_proj",
    )(ya, yb, yc, w_all, x, g.reshape(1, d))


def _mlp_kernel(h_ref, wu_ref, wd_ref, x_ref, g_ref, xo_ref, *rest, nj, emit_norm):
    j = pl.program_id(1)

    @pl.when(j == 0)
    def _():
        xo_ref[...] = x_ref[...]

    a = jnp.maximum(_dot(h_ref[...], wu_ref[...]), 0.0)
    a = (a * a).astype(BF16)
    xo_ref[...] += _dot(a, wd_ref[...])

    if emit_norm:
        xn_ref = rest[0]

        @pl.when(j == nj - 1)
        def _():
            xn_ref[...] = _rms_rows(xo_ref[...], g_ref[...]).astype(xn_ref.dtype)


def mlp(hn, wu_all, wd_all, layer, x, g, tm, tf, emit_norm):
    m, d = x.shape
    f = wu_all.shape[2]
    nj = f // tf
    row = lambda i, j: (i, 0)
    out_shape = [jax.ShapeDtypeStruct((m, d), F32)]
    out_specs = [pl.BlockSpec((tm, d), row)]
    if emit_norm:
        out_shape.append(jax.ShapeDtypeStruct((m, d), BF16))
        out_specs.append(pl.BlockSpec((tm, d), row))
    res = pl.pallas_call(
        functools.partial(_mlp_kernel, nj=nj, emit_norm=emit_norm),
        out_shape=tuple(out_shape),
        grid=(m // tm, nj),
        in_specs=[pl.BlockSpec((tm, d), row),
                  pl.BlockSpec((None, d, tf), lambda i, j: (layer, 0, j)),
                  pl.BlockSpec((None, tf, d), lambda i, j: (layer, j, 0)),
                  pl.BlockSpec((tm, d), row),
                  pl.BlockSpec((1, d), lambda i, j: (0, 0))],
        out_specs=tuple(out_specs),
        compiler_params=_cparams(("parallel", "arbitrary")),
        name="mlp",
    )(hn, wu_all, wd_all, x, g.reshape(1, d))
    return res if emit_norm else (res[0], None)


def _hg_constants(c):
    i = np.arange(c)[:, None]
    j = np.arange(c)[None, :]
    x = i ^ j
    lvl = np.where(x == 0, 0, np.floor(np.log2(np.maximum(x, 1))).astype(np.int64) + 1)
    tri_f = (j <= i).astype(np.float32)
    tri_r = (j >= i).astype(np.float32)
    lv_f = np.where(i >= j, lvl, -1).astype(np.int32)
    lv_r = np.where(i <= j, lvl, -1).astype(np.int32)
    return (jnp.asarray(tri_f, BF16), jnp.asarray(tri_r, BF16),
            jnp.asarray(lv_f), jnp.asarray(lv_r))


def _hg_ref_rows(b, m, reverse, rowid):
    c = b.shape[0]
    g = 2 * m
    r = m if reverse else m - 1
    if g >= 8:
        b3 = b.reshape(c // g, g, LANES)
        return jnp.broadcast_to(b3[:, r:r + 1, :], b3.shape).reshape(c, LANES)
    pos = rowid % g
    out = b
    for p in range(g):
        off = r - p
        if off == 0:
            continue
        out = jnp.where(pos == p, pltpu.roll(b, (-off) % c, 0), out)
    return out


def _hg_chunk(q, k, v, lf, tri, lv, rowid, st, reverse):
    c = q.shape[0]
    hi = lf.astype(BF16)
    lo = (lf - hi.astype(F32)).astype(BF16)
    b = _dot(tri, hi) + _dot(tri, lo)
    vb = v.astype(BF16)
    a = jnp.where(lv == 0, _dot_nt(q.astype(BF16), k.astype(BF16)), 0.0)
    for ell in range(1, int(math.log2(c)) + 1):
        m = 2 ** (ell - 1)
        fac = jnp.exp(-jnp.abs(b - _hg_ref_rows(b, m, reverse, rowid)))
        s = _dot_nt((q * fac).astype(BF16), (k * fac).astype(BF16))
        a = jnp.where(lv == ell, s, a)
    out = _dot(a.astype(BF16), vb) + _dot_nt((q * jnp.exp(b)).astype(BF16), st.astype(BF16))
    b_edge = b[0:1, :] if reverse else b[c - 1:c, :]
    kk = (k * jnp.exp(b_edge - b)).astype(BF16)
    st_new = st * jnp.exp(b_edge) + _dot_tn(vb, kk)
    return out, st_new


def _hgrn2_kernel(q_ref, v_ref, zf_ref, zb_ref, g_ref, lbf_ref, lbb_ref, ng_ref,
                  trif_ref, trir_ref, lvf_ref, lvr_ref, o_ref, st_ref, ob_ref, *, t_rows, n_t):
    t = pl.program_id(2)
    c = HG_CHUNK
    n_c = t_rows // c
    rowid = lax.broadcasted_iota(jnp.int32, (c, LANES), 0)

    def sweep(z_ref, lb_ref, tri_ref, lv_ref, reverse, blk):
        lb = lb_ref[...]
        tri = tri_ref[...]
        lv = lv_ref[...]

        def body(ci, carry):
            cc = (n_c - 1 - ci) if reverse else ci
            rows = pl.ds(pl.multiple_of(cc * c, c), c)
            q = q_ref[rows, :].astype(F32)
            v = v_ref[rows, :].astype(F32)
            f = lb + (1.0 - lb) * jax.nn.sigmoid(z_ref[rows, :].astype(F32))
            out, st_new = _hg_chunk(q, 1.0 - f, v, jnp.log(f), tri, lv, rowid,
                                    st_ref[...], reverse)
            st_ref[...] = st_new
            srows = pl.ds(pl.multiple_of(blk * t_rows + cc * c, c), c)
            if reverse:
                ob_ref[srows, :] = out
            else:
                o = out + ob_ref[srows, :]
                gate = g_ref[rows, :].astype(F32)
                y = _rms_rows(o, ng_ref[...]) * (gate * jax.nn.sigmoid(gate))
                o_ref[rows, :] = y.astype(o_ref.dtype)
            return carry

        lax.fori_loop(0, n_c, body, 0)

    @pl.when((t == 0) | (t == n_t))
    def _():
        st_ref[...] = jnp.zeros_like(st_ref)

    @pl.when(t < n_t)
    def _():
        sweep(zb_ref, lbb_ref, trir_ref, lvr_ref, True, n_t - 1 - t)

    @pl.when(t >= n_t)
    def _():
        sweep(zf_ref, lbf_ref, trif_ref, lvf_ref, False, t - n_t)


def hgrn2_mixer(h3, lb_f, lb_b, norm_g, t_rows):
    bsz, s, _ = h3.shape
    n_t = s // t_rows
    tri_f, tri_r, lv_f, lv_r = _hg_constants(HG_CHUNK)

    def both(cb):
        return pl.BlockSpec((None, t_rows, LANES),
                            lambda b, h, t: (b, jnp.where(t < n_t, n_t - 1 - t, t - n_t), cb + h))

    def bwd_only(cb):
        return pl.BlockSpec((None, t_rows, LANES),
                            lambda b, h, t: (b, jnp.maximum(n_t - 1 - t, 0), cb + h))

    def fwd_only(cb):
        return pl.BlockSpec((None, t_rows, LANES),
                            lambda b, h, t: (b, jnp.maximum(t - n_t, 0), cb + h))

    head_row = pl.BlockSpec((None, 1, LANES), lambda b, h, t: (h, 0, 0))
    const = pl.BlockSpec((HG_CHUNK, HG_CHUNK), lambda b, h, t: (0, 0))
    return pl.pallas_call(
        functools.partial(_hgrn2_kernel, t_rows=t_rows, n_t=n_t),
        out_shape=jax.ShapeDtypeStruct((bsz, s, HG_W), BF16),
        grid=(bsz, HG_HEADS, 2 * n_t),
        in_specs=[both(CB_HG_Q), both(CB_HG_V), fwd_only(CB_HG_ZF), bwd_only(CB_HG_ZB),
                  fwd_only(CB_HG_G), head_row, head_row, head_row, const, const, const, const],
        out_specs=pl.BlockSpec((None, t_rows, LANES),
                               lambda b, h, t: (b, jnp.maximum(t - n_t, 0), h)),
        scratch_shapes=[pltpu.VMEM((LANES, LANES), F32), pltpu.VMEM((s, LANES), F32)],
        compiler_params=_cparams(("parallel", "parallel", "arbitrary")),
        name="hgrn2",
    )(h3, h3, h3, h3, h3, lb_f.reshape(HG_HEADS, 1, LANES), lb_b.reshape(HG_HEADS, 1, LANES),
      norm_g.reshape(HG_HEADS, 1, LANES), tri_f, tri_r, lv_f, lv_r)


def _rope_tables(s):
    half = RET_DK // 2
    inv = ROPE_BASE ** (-jnp.arange(half, dtype=F32) / half)
    ang = jnp.arange(s, dtype=F32)[:, None] * inv[None, :]
    cos, sin = jnp.cos(ang), jnp.sin(ang)
    cos_t = jnp.concatenate([cos, cos, cos, cos], axis=-1)
    sin_t = jnp.concatenate([-sin, sin, -sin, sin], axis=-1)
    return cos_t, sin_t


def _rope_pair(x, cos_t, sin_t, lane):
    half = RET_DK // 2
    swapped = jnp.where(lane % RET_DK < half, pltpu.roll(x, LANES - half, 1), pltpu.roll(x, half, 1))
    return x * cos_t + swapped * sin_t


def _retention_kernel(lg_ref, q_ref, k_ref, v_ref, g_ref, cos_ref, sin_ref, ng_ref, o_ref,
                      sb_ref, sbc_ref, sf_ref, d_ref, *, t_rows, n_t):
    pair = pl.program_id(1)
    t = pl.program_id(2)
    tr = t_rows
    lane = lax.broadcasted_iota(jnp.int32, (tr, LANES), 1)
    row = lax.broadcasted_iota(jnp.int32, (tr, LANES), 0).astype(F32)
    cos_t = cos_ref[...]
    sin_t = sin_ref[...]
    k = _rope_pair(k_ref[...].astype(F32), cos_t, sin_t, lane) * (RET_DK ** -0.5)

    @pl.when(t == 0)
    def _():
        sbc_ref[...] = jnp.zeros_like(sbc_ref)

    @pl.when(t == n_t)
    def _():
        sf_ref[...] = jnp.zeros_like(sf_ref)
        ri = lax.broadcasted_iota(jnp.int32, (tr, tr), 0)
        ci = lax.broadcasted_iota(jnp.int32, (tr, tr), 1)
        rel = (ri - ci).astype(F32)
        for hh in range(2):
            lg_f = lg_ref[0, 2 * pair + hh]
            lg_b = lg_ref[1, 2 * pair + hh]
            d_ref[hh] = (jnp.exp(lg_f * jnp.maximum(rel, 0.0) + lg_b * jnp.maximum(-rel, 0.0))
                         + jnp.where(rel == 0.0, 1.0, 0.0))

    @pl.when(t < n_t)
    def _():
        blk = n_t - 1 - t
        for hh in range(2):
            lg_b = lg_ref[1, 2 * pair + hh]
            km = jnp.where(lane // RET_DK == hh, k, 0.0)
            vb = v_ref[:, hh * LANES:(hh + 1) * LANES].astype(BF16)
            cur = sbc_ref[hh]
            sb_ref[hh, blk] = cur
            zeta = jnp.exp(lg_b * row)
            sbc_ref[hh] = cur * jnp.exp(lg_b * tr) + _dot_tn((km * zeta).astype(BF16), vb)

    @pl.when(t >= n_t)
    def _():
        blk = t - n_t
        q = _rope_pair(q_ref[...].astype(F32), cos_t, sin_t, lane)
        qb = q.astype(BF16)
        for hh in range(2):
            lg_f = lg_ref[0, 2 * pair + hh]
            lg_b = lg_ref[1, 2 * pair + hh]
            km = jnp.where(lane // RET_DK == hh, k, 0.0)
            vb = v_ref[:, hh * LANES:(hh + 1) * LANES].astype(BF16)
            p = (_dot_nt(qb, km.astype(BF16)) * d_ref[hh]).astype(BF16)
            sf = sf_ref[hh]
            o = (_dot(p, vb)
                 + _dot((q * jnp.exp(lg_f * (row + 1.0))).astype(BF16), sf.astype(BF16))
                 + _dot((q * jnp.exp(lg_b * (tr - row))).astype(BF16), sb_ref[hh, blk].astype(BF16)))
            zeta = jnp.exp(lg_f * (tr - 1.0 - row))
            sf_ref[hh] = sf * jnp.exp(lg_f * tr) + _dot_tn((km * zeta).astype(BF16), vb)
            mu = jnp.mean(o, axis=-1, keepdims=True)
            oc = o - mu
            var = jnp.mean(oc * oc, axis=-1, keepdims=True)
            gate = g_ref[:, hh * LANES:(hh + 1) * LANES].astype(F32)
            y = oc * lax.rsqrt(var + EPS) * ng_ref[:, hh * LANES:(hh + 1) * LANES]
            o_ref[:, hh * LANES:(hh + 1) * LANES] = (y * (gate * jax.nn.sigmoid(gate))).astype(o_ref.dtype)


def retention_mixer(h3, norm_g, t_rows):
    bsz, s, _ = h3.shape
    n_t = s // t_rows
    cos_t, sin_t = _rope_tables(s)
    hidx = jnp.arange(RET_HEADS, dtype=F32)
    lg_f = jnp.log1p(-jnp.exp2(-5.0 - hidx))
    lg = jnp.stack([lg_f, lg_f[::-1]], axis=0)
    blk_of = lambda t: jnp.where(t < n_t, n_t - 1 - t, t - n_t)
    fwd_blk = lambda t: jnp.maximum(t - n_t, 0)
    return pl.pallas_call(
        functools.partial(_retention_kernel, t_rows=t_rows, n_t=n_t),
        out_shape=jax.ShapeDtypeStruct((bsz, s, RET_W), BF16),
        grid=(bsz, RET_HEADS // 2, 2 * n_t),
        in_specs=[pl.BlockSpec(memory_space=pltpu.SMEM),
                  pl.BlockSpec((None, t_rows, LANES), lambda b, p, t: (b, fwd_blk(t), CB_RET_Q + p)),
                  pl.BlockSpec((None, t_rows, LANES), lambda b, p, t: (b, blk_of(t), CB_RET_K + p)),
                  pl.BlockSpec((None, t_rows, 2 * LANES), lambda b, p, t: (b, blk_of(t), CB_RET_V // 2 + p)),
                  pl.BlockSpec((None, t_rows, 2 * LANES), lambda b, p, t: (b, fwd_blk(t), CB_RET_G // 2 + p)),
                  pl.BlockSpec((t_rows, LANES), lambda b, p, t: (blk_of(t), 0)),
                  pl.BlockSpec((t_rows, LANES), lambda b, p, t: (blk_of(t), 0)),
                  pl.BlockSpec((1, 2 * LANES), lambda b, p, t: (0, p))],
        out_specs=pl.BlockSpec((None, t_rows, 2 * LANES), lambda b, p, t: (b, fwd_blk(t), p)),
        scratch_shapes=[pltpu.VMEM((2, n_t, LANES, LANES), F32),
                        pltpu.VMEM((2, LANES, LANES), F32),
                        pltpu.VMEM((2, LANES, LANES), F32),
                        pltpu.VMEM((2, t_rows, t_rows), F32)],
        compiler_params=_cparams(("parallel", "parallel", "arbitrary")),
        name="retention",
    )(lg, h3, h3, h3, h3, cos_t, sin_t, norm_g.reshape(1, RET_W))


ATT_QB = 128


def _t5_bucket(rel):
    nb = REL_BUCKETS // 2
    max_exact = nb // 2
    sign_off = jnp.where(rel > 0, nb, 0)
    n = jnp.abs(rel)
    nf = jnp.maximum(n, 1).astype(F32)
    large = max_exact + (jnp.log(nf / max_exact) / math.log(REL_MAX_DIST / max_exact)
                         * (nb - max_exact)).astype(jnp.int32)
    large = jnp.minimum(large, nb - 1)
    return sign_off + jnp.where(n < max_exact, n, large)


def _attn_kernel(tbl_ref, q_ref, kp_ref, k_ref, kn_ref, vp_ref, v_ref, vn_ref, qg_ref, kg_ref,
                 bkt_ref, o_ref, lse_ref, *, lq, l_total, head0):
    slot = pl.program_id(2)
    n = pl.program_id(3)
    kw = ATT_QB + 2 * DIL_HALF
    bkt = bkt_ref[...]
    bias = jnp.zeros((ATT_QB, kw), F32)
    for bucket in range(REL_BUCKETS):
        bias = jnp.where(bkt == bucket, tbl_ref[bucket, head0 + slot], bias)
    ii = lax.broadcasted_iota(jnp.int32, (ATT_QB, kw), 0)
    jj = lax.broadcasted_iota(jnp.int32, (ATT_QB, kw), 1)
    in_band = jnp.abs(jj - DIL_HALF - ii) <= DIL_HALF

    q = _rms_rows(q_ref[...].astype(F32), qg_ref[...]) * (LANES ** -0.5)
    kall = jnp.concatenate([kp_ref[...], k_ref[...], kn_ref[...]], axis=0).astype(F32)
    kall = _rms_rows(kall, kg_ref[...]).astype(BF16)
    vall = jnp.concatenate([vp_ref[...], v_ref[...], vn_ref[...]], axis=0).astype(BF16)
    qb16 = q.astype(BF16)
    for qb in range(lq // ATT_QB):
        r0 = qb * ATT_QB
        key_idx = n * lq + r0 - DIL_HALF + jj
        valid = in_band & (key_idx >= 0) & (key_idx < l_total)
        s = _dot_nt(qb16[r0:r0 + ATT_QB], kall[r0:r0 + kw]) + bias
        s = jnp.where(valid, s, NEG_BIG)
        m = jnp.max(s, axis=-1, keepdims=True)
        p = jnp.exp(s - m)
        den = jnp.sum(p, axis=-1, keepdims=True)
        o = _dot(p.astype(BF16), vall[r0:r0 + kw]) / den
        o_ref[r0:r0 + ATT_QB, :] = o
        lse_ref[r0:r0 + ATT_QB, :] = jnp.broadcast_to(m + jnp.log(den), (ATT_QB, LANES))


def dilated_group(hq_all, bsz, rel_bias, q_gain, k_gain, g, lq_max):
    m = hq_all.shape[1]
    s = m // bsz
    dil = DIL_GROUPS[g][1]
    l_total = s // dil
    lq = min(l_total, lq_max)
    n_blk = l_total // lq
    hpb = lq // DIL_HALF
    n_halo = l_total // DIL_HALF
    if dil == 1:
        hr, lead = hq_all.reshape(-1, bsz, 1, s, 3 * DIL_W), g
    else:
        hr = hq_all[g].reshape(bsz, l_total, dil, 3 * DIL_W).transpose(0, 2, 1, 3)[None]
        lead = 0
    cq, ck, cv = 0, DIL_SLOTS, 2 * DIL_SLOTS

    def main(cb):
        return pl.BlockSpec((None, None, None, lq, LANES), lambda b, r, sl, n: (lead, b, r, n, cb + sl))

    def prev(cb):
        return pl.BlockSpec((None, None, None, DIL_HALF, LANES),
                            lambda b, r, sl, n: (lead, b, r, jnp.maximum(n * hpb - 1, 0), cb + sl))

    def nxt(cb):
        return pl.BlockSpec((None, None, None, DIL_HALF, LANES),
                            lambda b, r, sl, n: (lead, b, r, jnp.minimum((n + 1) * hpb, n_halo - 1), cb + sl))

    ii = jnp.arange(ATT_QB)[:, None]
    jj = jnp.arange(ATT_QB + 2 * DIL_HALF)[None, :]
    bkt = _t5_bucket((jj - DIL_HALF - ii) * dil).astype(jnp.int32)
    gain = pl.BlockSpec((1, LANES), lambda b, r, sl, n: (0, 0))
    out_spec = pl.BlockSpec((None, None, lq, LANES), lambda b, r, sl, n: (b, r, n, sl))
    o, lse = pl.pallas_call(
        functools.partial(_attn_kernel, lq=lq, l_total=l_total, head0=g * DIL_SLOTS),
        out_shape=(jax.ShapeDtypeStruct((bsz, dil, l_total, DIL_W), F32),
                   jax.ShapeDtypeStruct((bsz, dil, l_total, DIL_W), F32)),
        grid=(bsz, dil, DIL_SLOTS, n_blk),
        in_specs=[pl.BlockSpec(memory_space=pltpu.SMEM),
                  main(cq), prev(ck), main(ck), nxt(ck), prev(cv), main(cv), nxt(cv),
                  gain, gain,
                  pl.BlockSpec(bkt.shape, lambda b, r, sl, n: (0, 0))],
        out_specs=(out_spec, out_spec),
        compiler_params=_cparams(("parallel", "parallel", "parallel", "arbitrary")),
        name=f"dilated_attn_g{g}",
    )(rel_bias, hr, hr, hr, hr, hr, hr, hr, q_gain.reshape(1, LANES), k_gain.reshape(1, LANES), bkt)
    back = lambda t: t.transpose(0, 2, 1, 3).reshape(m, DIL_W)
    return back(o), back(lse)


def _combine_kernel(o0_ref, o1_ref, o2_ref, l0_ref, l1_ref, l2_ref, y_ref):
    l0, l1, l2 = l0_ref[...], l1_ref[...], l2_ref[...]
    m = jnp.maximum(jnp.maximum(l0, l1), l2)
    w0, w1, w2 = jnp.exp(l0 - m), jnp.exp(l1 - m), jnp.exp(l2 - m)
    y = (w0 * o0_ref[...] + w1 * o1_ref[...] + w2 * o2_ref[...]) / (w0 + w1 + w2)
    y_ref[...] = y.astype(y_ref.dtype)


def dilated_mixer(hq_all, bsz, rel_bias, q_gain, k_gain, lq_max, tm):
    outs = [dilated_group(hq_all, bsz, rel_bias, q_gain, k_gain, g, lq_max)
            for g in range(len(DIL_GROUPS))]
    m = outs[0][0].shape[0]
    spec = pl.BlockSpec((tm, DIL_W), lambda i: (i, 0))
    return pl.pallas_call(
        _combine_kernel,
        out_shape=jax.ShapeDtypeStruct((m, DIL_W), BF16),
        grid=(m // tm,),
        in_specs=[spec] * 6,
        out_specs=spec,
        compiler_params=_cparams(("parallel",)),
        name="dilated_combine",
    )(outs[0][0], outs[1][0], outs[2][0], outs[0][1], outs[1][1], outs[2][1])


def _tiles(bsz, s):
    m = bsz * s
    return dict(tm=min(1024, m), tm_small=min(512, m), hg_t=min(512, s), ret_t=min(512, s),
                lq_max=1024)


def kernel(x, w_in, w_out, w_up, w_down, norm_mix, norm_mlp, hg_lb_fwd, hg_lb_bwd,
           hg_norm, ret_norm, q_norm, k_norm, rel_bias):
    bsz, s, d = x.shape
    m = bsz * s
    tl = _tiles(bsz, s)
    lb_f = jnp.cumsum(jax.nn.softmax(hg_lb_fwd.astype(F32), axis=0), axis=0)
    lb_b = jnp.cumsum(jax.nn.softmax(hg_lb_bwd.astype(F32), axis=0), axis=0)
    lb_f = lb_f - lb_f[0]
    lb_b = lb_b - lb_b[0]
    w_in_b, w_out_b = w_in.astype(BF16), w_out.astype(BF16)
    w_up_b, w_down_b = w_up.astype(BF16), w_down.astype(BF16)

    xf = x.reshape(m, d)
    xn = rmsnorm(xf, norm_mix[0], tl["tm_small"])
    mix_cols = CB_DIL * LANES
    for l in range(DEPTH):
        h3 = in_proj(xn, w_in_b, l, tl["tm"], 1536, 0, mix_cols).reshape(bsz, s, mix_cols)
        hq_all = in_proj_groups(xn, w_in_b, l, tl["tm"], mix_cols)
        ya = hgrn2_mixer(h3, lb_f[l], lb_b[l], hg_norm[l], tl["hg_t"]).reshape(m, HG_W)
        yb = retention_mixer(h3, ret_norm[l], tl["ret_t"]).reshape(m, RET_W)
        yc = dilated_mixer(hq_all, bsz, rel_bias, q_norm[l], k_norm[l], tl["lq_max"], tl["tm_small"])
        xf, hn = out_proj(ya, yb, yc, w_out_b, l, xf, norm_mlp[l], tl["tm_small"])
        last = l == DEPTH - 1
        xf, xn = mlp(hn, w_up_b, w_down_b, l, xf, norm_mix[(l + 1) % DEPTH],
                     tl["tm_small"], 1024, not last)
    return xf.reshape(bsz, s, d)
```

```python
import functools
import math

import numpy as np
import jax
import jax.numpy as jnp
from jax import lax
from jax.experimental import pallas as pl
from jax.experimental.pallas import tpu as pltpu

F32 = jnp.float32
BF16 = jnp.bfloat16
EPS = 1e-6

D_MODEL = 2048
DEPTH = 4
LANES = 128

HG_HEADS = 6
HG_W = HG_HEADS * LANES
RET_HEADS = 6
RET_DK = 64
RET_W = RET_HEADS * LANES
ROPE_BASE = 10000.0
DIL_SLOTS = 4
DIL_GROUPS = ((128, 1), (512, 4), (2048, 16))
DIL_W = DIL_SLOTS * LANES
DIL_HALF = 64
MIX_W = HG_W + RET_W + DIL_W
D_FF = 4 * D_MODEL
REL_BUCKETS = 32
REL_MAX_DIST = 1024

CB_HG_Q, CB_HG_V, CB_HG_ZF, CB_HG_ZB, CB_HG_G = 0, 6, 12, 18, 24
CB_RET_Q, CB_RET_K = 30, 33
CB_RET_V, CB_RET_G = 36, 42
CB_DIL = 48
IN_W = 10752
IN_CB = IN_W // LANES

H_DTYPE = BF16
VMEM_LIMIT = 56 * 1024 * 1024

HG_CHUNK = 128
HG_PAR = 3
LOG2E = 1.4426950408889634
NEG_BIG = -1e30


def _cparams(sem):
    return pltpu.CompilerParams(dimension_semantics=sem, vmem_limit_bytes=VMEM_LIMIT)


def _dot(a, b):
    return jnp.dot(a, b, preferred_element_type=F32)


def _dot_nt(a, b):
    return lax.dot_general(a, b, (((1,), (1,)), ((), ())), preferred_element_type=F32)


def _dot_tn(a, b):
    return lax.dot_general(a, b, (((0,), (0,)), ((), ())), preferred_element_type=F32)


def _rms_rows(x, g):
    return x * lax.rsqrt(jnp.mean(x * x, axis=-1, keepdims=True) + EPS) * g


def _rmsnorm_kernel(x_ref, g_ref, o_ref):
    o_ref[...] = _rms_rows(x_ref[...], g_ref[...]).astype(o_ref.dtype)


def rmsnorm(x, g, tm):
    m, d = x.shape
    return pl.pallas_call(
        _rmsnorm_kernel,
        out_shape=jax.ShapeDtypeStruct((m, d), BF16),
        grid=(m // tm,),
        in_specs=[pl.BlockSpec((tm, d), lambda i: (i, 0)),
                  pl.BlockSpec((1, d), lambda i: (0, 0))],
        out_specs=pl.BlockSpec((tm, d), lambda i: (i, 0)),
        compiler_params=_cparams(("parallel",)),
        name="rmsnorm",
    )(x, g.reshape(1, d))


def _matmul_kernel(x_ref, w_ref, o_ref):
    o_ref[...] = _dot(x_ref[...], w_ref[...]).astype(o_ref.dtype)


def in_proj(xn, w_all, layer, tm, tn, col0, ncols):
    m, d = xn.shape
    j0 = col0 // tn
    return pl.pallas_call(
        _matmul_kernel,
        out_shape=jax.ShapeDtypeStruct((m, ncols), H_DTYPE),
        grid=(m // tm, ncols // tn),
        in_specs=[pl.BlockSpec((tm, d), lambda i, j: (i, 0)),
                  pl.BlockSpec((None, d, tn), lambda i, j: (layer, 0, j0 + j))],
        out_specs=pl.BlockSpec((tm, tn), lambda i, j: (i, j)),
        compiler_params=_cparams(("parallel", "arbitrary")),
        name="in_proj",
    )(xn, w_all)


def in_proj_groups(xn, w_all, layer, tm, col0):
    m, d = xn.shape
    tn = 3 * DIL_W
    j0 = col0 // tn
    return pl.pallas_call(
        _matmul_kernel,
        out_shape=jax.ShapeDtypeStruct((len(DIL_GROUPS), m, tn), H_DTYPE),
        grid=(m // tm, len(DIL_GROUPS)),
        in_specs=[pl.BlockSpec((tm, d), lambda i, j: (i, 0)),
                  pl.BlockSpec((None, d, tn), lambda i, j: (layer, 0, j0 + j))],
        out_specs=pl.BlockSpec((None, tm, tn), lambda i, j: (j, i, 0)),
        compiler_params=_cparams(("parallel", "arbitrary")),
        name="in_proj_groups",
    )(xn, w_all)


def _out_proj_kernel(ya_ref, yb_ref, yc_ref, w_ref, x_ref, g_ref, xo_ref, hn_ref):
    y = (_dot(ya_ref[...], w_ref[0:HG_W, :])
         + _dot(yb_ref[...], w_ref[HG_W:HG_W + RET_W, :])
         + _dot(yc_ref[...], w_ref[HG_W + RET_W:MIX_W, :]))
    xo = x_ref[...] + y
    xo_ref[...] = xo
    hn_ref[...] = _rms_rows(xo, g_ref[...]).astype(hn_ref.dtype)


def out_proj(ya, yb, yc, w_all, layer, x, g, tm):
    m, d = x.shape
    row = lambda i: (i, 0)
    return pl.pallas_call(
        _out_proj_kernel,
        out_shape=(jax.ShapeDtypeStruct((m, d), F32), jax.ShapeDtypeStruct((m, d), BF16)),
        grid=(m // tm,),
        in_specs=[pl.BlockSpec((tm, HG_W), row), pl.BlockSpec((tm, RET_W), row),
                  pl.BlockSpec((tm, DIL_W), row),
                  pl.BlockSpec((None, MIX_W, d), lambda i: (layer, 0, 0)),
                  pl.BlockSpec((tm, d), row),
                  pl.BlockSpec((1, d), lambda i: (0, 0))],
        out_specs=(pl.BlockSpec((tm, d), row), pl.BlockSpec((tm, d), row)),
        compiler_params=_cparams(("parallel",)),
        name="out_proj",
    )(ya, yb, yc, w_all, x, g.reshape(1, d))


def _mlp_kernel(h_ref, wu_ref, wd_ref, x_ref, g_ref, xo_ref, *rest, nj, emit_norm):
    j = pl.program_id(1)

    @pl.when(j == 0)
    def _():
        xo_ref[...] = x_ref[...]

    a = jnp.maximum(_dot(h_ref[...], wu_ref[...]), 0.0)
    a = (a * a).astype(BF16)
    xo_ref[...] += _dot(a, wd_ref[...])

    if emit_norm:
        xn_ref = rest[0]

        @pl.when(j == nj - 1)
        def _():
            xn_ref[...] = _rms_rows(xo_ref[...], g_ref[...]).astype(xn_ref.dtype)


def mlp(hn, wu_all, wd_all, layer, x, g, tm, tf, emit_norm):
    m, d = x.shape
    f = wu_all.shape[2]
    nj = f // tf
    row = lambda i, j: (i, 0)
    out_shape = [jax.ShapeDtypeStruct((m, d), F32)]
    out_specs = [pl.BlockSpec((tm, d), row)]
    if emit_norm:
        out_shape.append(jax.ShapeDtypeStruct((m, d), BF16))
        out_specs.append(pl.BlockSpec((tm, d), row))
    res = pl.pallas_call(
        functools.partial(_mlp_kernel, nj=nj, emit_norm=emit_norm),
        out_shape=tuple(out_shape),
        grid=(m // tm, nj),
        in_specs=[pl.BlockSpec((tm, d), row),
                  pl.BlockSpec((None, d, tf), lambda i, j: (layer, 0, j)),
                  pl.BlockSpec((None, tf, d), lambda i, j: (layer, j, 0)),
                  pl.BlockSpec((tm, d), row),
                  pl.BlockSpec((1, d), lambda i, j: (0, 0))],
        out_specs=tuple(out_specs),
        compiler_params=_cparams(("parallel", "arbitrary")),
        name="mlp",
    )(hn, wu_all, wd_all, x, g.reshape(1, d))
    return res if emit_norm else (res[0], None)


def _hg_constants(c):
    i = np.arange(c)[:, None]
    j = np.arange(c)[None, :]
    x = i ^ j
    lvl = np.where(x == 0, 0, np.floor(np.log2(np.maximum(x, 1))).astype(np.int64) + 1)
    tri_f = (j <= i).astype(np.float32)
    tri_r = (j >= i).astype(np.float32)
    lv_f = np.where(i >= j, lvl, -1).astype(np.int32)
    lv_r = np.where(i <= j, lvl, -1).astype(np.int32)
    return (jnp.asarray(tri_f, BF16), jnp.asarray(tri_r, BF16),
            jnp.asarray(lv_f), jnp.asarray(lv_r))


def _hg_ref_rows(b, m, reverse, rowid):
    c = b.shape[0]
    g = 2 * m
    r = m if reverse else m - 1
    if g >= 8:
        b3 = b.reshape(c // g, g, LANES)
        return jnp.broadcast_to(b3[:, r:r + 1, :], b3.shape).reshape(c, LANES)
    pos = rowid % g
    out = b
    for p in range(g):
        off = r - p
        if off == 0:
            continue
        out = jnp.where(pos == p, pltpu.roll(b, (-off) % c, 0), out)
    return out


def _neg_abs(x):
    bits = lax.bitcast_convert_type(x, jnp.uint32) | jnp.uint32(0x80000000)
    return lax.bitcast_convert_type(bits, F32)


def _hg_chunk(q, k, v, lf2, tri, lv, rowid, st, reverse):
    c = q.shape[0]
    hi = lf2.astype(BF16)
    lo = (lf2 - hi.astype(F32)).astype(BF16)
    b = _dot(tri, hi) + _dot(tri, lo)
    vb = v.astype(BF16)
    qb, kb = q.astype(BF16), k.astype(BF16)
    a = jnp.where(lv == 0, _dot_nt(qb, kb), 0.0)
    for ell in range(1, int(math.log2(c)) + 1):
        m = 2 ** (ell - 1)
        fac = jnp.exp2(_neg_abs(b - _hg_ref_rows(b, m, reverse, rowid)))
        a = jnp.where(lv == ell, _dot_nt((q * fac).astype(BF16), (k * fac).astype(BF16)), a)
    out = _dot(a.astype(BF16), vb) + _dot_nt((q * jnp.exp2(b)).astype(BF16), st.astype(BF16))
    b_edge = b[0:1, :] if reverse else b[c - 1:c, :]
    kk = (k * jnp.exp2(b_edge - b)).astype(BF16)
    st_new = st * jnp.exp2(b_edge) + _dot_tn(vb, kk)
    return out, st_new


def _hgrn2_kernel(q_ref, v_ref, zf_ref, zb_ref, g_ref, lbf_ref, lbb_ref, ng_ref,
                  trif_ref, trir_ref, lvf_ref, lvr_ref, o_ref, st_ref, ob_ref, *, t_rows, n_t):
    t = pl.program_id(2)
    c = HG_CHUNK
    n_c = t_rows // c
    rowid = lax.broadcasted_iota(jnp.int32, (c, LANES), 0)

    def sweep(z_ref, lb_ref, tri_ref, lv_ref, reverse, blk):
        tri = tri_ref[...]
        lv = lv_ref[...]

        def body(ci, carry):
            cc = (n_c - 1 - ci) if reverse else ci
            rows = pl.ds(pl.multiple_of(cc * c, c), c)
            srows = pl.ds(pl.multiple_of(blk * t_rows + cc * c, c), c)
            for hh in range(HG_PAR):
                cols = slice(hh * LANES, (hh + 1) * LANES)
                lb = lb_ref[:, cols]
                q = q_ref[rows, cols].astype(F32)
                v = v_ref[rows, cols].astype(F32)
                f = lb + (1.0 - lb) * jax.nn.sigmoid(z_ref[rows, cols].astype(F32))
                out, st_new = _hg_chunk(q, 1.0 - f, v, jnp.log(f) * LOG2E, tri, lv, rowid,
                                        st_ref[hh], reverse)
                st_ref[hh] = st_new
                if reverse:
                    ob_ref[srows, cols] = out
                else:
                    o = out + ob_ref[srows, cols]
                    gate = g_ref[rows, cols].astype(F32)
                    y = _rms_rows(o, ng_ref[:, cols]) * (gate * jax.nn.sigmoid(gate))
                    o_ref[rows, cols] = y.astype(o_ref.dtype)
            return carry

        lax.fori_loop(0, n_c, body, 0, unroll=True)

    @pl.when((t == 0) | (t == n_t))
    def _():
        st_ref[...] = jnp.zeros_like(st_ref)

    @pl.when(t < n_t)
    def _():
        sweep(zb_ref, lbb_ref, trir_ref, lvr_ref, True, n_t - 1 - t)

    @pl.when(t >= n_t)
    def _():
        sweep(zf_ref, lbf_ref, trif_ref, lvf_ref, False, t - n_t)


def hgrn2_mixer(h3, lb_f, lb_b, norm_g, t_rows):
    bsz, s, _ = h3.shape
    n_t = s // t_rows
    w = HG_PAR * LANES
    tri_f, tri_r, lv_f, lv_r = _hg_constants(HG_CHUNK)

    def both(cb):
        return pl.BlockSpec((None, t_rows, w),
                            lambda b, h, t: (b, jnp.where(t < n_t, n_t - 1 - t, t - n_t), cb // HG_PAR + h))

    def bwd_only(cb):
        return pl.BlockSpec((None, t_rows, w),
                            lambda b, h, t: (b, jnp.maximum(n_t - 1 - t, 0), cb // HG_PAR + h))

    def fwd_only(cb):
        return pl.BlockSpec((None, t_rows, w),
                            lambda b, h, t: (b, jnp.maximum(t - n_t, 0), cb // HG_PAR + h))

    head_row = pl.BlockSpec((1, w), lambda b, h, t: (0, h))
    const = pl.BlockSpec((HG_CHUNK, HG_CHUNK), lambda b, h, t: (0, 0))
    return pl.pallas_call(
        functools.partial(_hgrn2_kernel, t_rows=t_rows, n_t=n_t),
        out_shape=jax.ShapeDtypeStruct((bsz, s, HG_W), BF16),
        grid=(bsz, HG_HEADS // HG_PAR, 2 * n_t),
        in_specs=[both(CB_HG_Q), both(CB_HG_V), fwd_only(CB_HG_ZF), bwd_only(CB_HG_ZB),
                  fwd_only(CB_HG_G), head_row, head_row, head_row, const, const, const, const],
        out_specs=pl.BlockSpec((None, t_rows, w), lambda b, h, t: (b, jnp.maximum(t - n_t, 0), h)),
        scratch_shapes=[pltpu.VMEM((HG_PAR, LANES, LANES), F32), pltpu.VMEM((s, w), F32)],
        compiler_params=_cparams(("parallel", "parallel", "arbitrary")),
        name="hgrn2",
    )(h3, h3, h3, h3, h3, lb_f.reshape(1, HG_W), lb_b.reshape(1, HG_W),
      norm_g.reshape(1, HG_W), tri_f, tri_r, lv_f, lv_r)


def _rope_tables(s):
    half = RET_DK // 2
    inv = ROPE_BASE ** (-jnp.arange(half, dtype=F32) / half)
    ang = jnp.arange(s, dtype=F32)[:, None] * inv[None, :]
    cos, sin = jnp.cos(ang), jnp.sin(ang)
    cos_t = jnp.concatenate([cos, cos, cos, cos], axis=-1)
    sin_t = jnp.concatenate([-sin, sin, -sin, sin], axis=-1)
    return cos_t, sin_t


def _rope_pair(x, cos_t, sin_t, lane):
    half = RET_DK // 2
    swapped = jnp.where(lane % RET_DK < half, pltpu.roll(x, LANES - half, 1), pltpu.roll(x, half, 1))
    return x * cos_t + swapped * sin_t


def _retention_kernel(lg_ref, q_ref, k_ref, v_ref, g_ref, cos_ref, sin_ref, ng_ref, o_ref,
                      sb_ref, sbc_ref, sf_ref, d_ref, *, t_rows, n_t):
    pair = pl.program_id(1)
    t = pl.program_id(2)
    tr = t_rows
    lane = lax.broadcasted_iota(jnp.int32, (tr, LANES), 1)
    row = lax.broadcasted_iota(jnp.int32, (tr, LANES), 0).astype(F32)
    cos_t = cos_ref[...]
    sin_t = sin_ref[...]
    k = _rope_pair(k_ref[...].astype(F32), cos_t, sin_t, lane) * (RET_DK ** -0.5)

    @pl.when(t == 0)
    def _():
        sbc_ref[...] = jnp.zeros_like(sbc_ref)

    @pl.when(t == n_t)
    def _():
        sf_ref[...] = jnp.zeros_like(sf_ref)
        ri = lax.broadcasted_iota(jnp.int32, (tr, tr), 0)
        ci = lax.broadcasted_iota(jnp.int32, (tr, tr), 1)
        rel = (ri - ci).astype(F32)
        for hh in range(2):
            lg_f = lg_ref[0, 2 * pair + hh]
            lg_b = lg_ref[1, 2 * pair + hh]
            d_ref[hh] = (jnp.exp(lg_f * jnp.maximum(rel, 0.0) + lg_b * jnp.maximum(-rel, 0.0))
                         + jnp.where(rel == 0.0, 1.0, 0.0))

    @pl.when(t < n_t)
    def _():
        blk = n_t - 1 - t
        for hh in range(2):
            lg_b = lg_ref[1, 2 * pair + hh]
            km = jnp.where(lane // RET_DK == hh, k, 0.0)
            vb = v_ref[:, hh * LANES:(hh + 1) * LANES].astype(BF16)
            cur = sbc_ref[hh]
            sb_ref[hh, blk] = cur
            zeta = jnp.exp(lg_b * row)
            sbc_ref[hh] = cur * jnp.exp(lg_b * tr) + _dot_tn((km * zeta).astype(BF16), vb)

    @pl.when(t >= n_t)
    def _():
        blk = t - n_t
        q = _rope_pair(q_ref[...].astype(F32), cos_t, sin_t, lane)
        qb = q.astype(BF16)
        for hh in range(2):
            lg_f = lg_ref[0, 2 * pair + hh]
            lg_b = lg_ref[1, 2 * pair + hh]
            km = jnp.where(lane // RET_DK == hh, k, 0.0)
            vb = v_ref[:, hh * LANES:(hh + 1) * LANES].astype(BF16)
            p = (_dot_nt(qb, km.astype(BF16)) * d_ref[hh]).astype(BF16)
            sf = sf_ref[hh]
            o = (_dot(p, vb)
                 + _dot((q * jnp.exp(lg_f * (row + 1.0))).astype(BF16), sf.astype(BF16))
                 + _dot((q * jnp.exp(lg_b * (tr - row))).astype(BF16), sb_ref[hh, blk].astype(BF16)))
            zeta = jnp.exp(lg_f * (tr - 1.0 - row))
            sf_ref[hh] = sf * jnp.exp(lg_f * tr) + _dot_tn((km * zeta).astype(BF16), vb)
            mu = jnp.mean(o, axis=-1, keepdims=True)
            oc = o - mu
            var = jnp.mean(oc * oc, axis=-1, keepdims=True)
            gate = g_ref[:, hh * LANES:(hh + 1) * LANES].astype(F32)
            y = oc * lax.rsqrt(var + EPS) * ng_ref[:, hh * LANES:(hh + 1) * LANES]
            o_ref[:, hh * LANES:(hh + 1) * LANES] = (y * (gate * jax.nn.sigmoid(gate))).astype(o_ref.dtype)


def retention_mixer(h3, norm_g, t_rows):
    bsz, s, _ = h3.shape
    n_t = s // t_rows
    cos_t, sin_t = _rope_tables(s)
    hidx = jnp.arange(RET_HEADS, dtype=F32)
    lg_f = jnp.log1p(-jnp.exp2(-5.0 - hidx))
    lg = jnp.stack([lg_f, lg_f[::-1]], axis=0)
    blk_of = lambda t: jnp.where(t < n_t, n_t - 1 - t, t - n_t)
    fwd_blk = lambda t: jnp.maximum(t - n_t, 0)
    return pl.pallas_call(
        functools.partial(_retention_kernel, t_rows=t_rows, n_t=n_t),
        out_shape=jax.ShapeDtypeStruct((bsz, s, RET_W), BF16),
        grid=(bsz, RET_HEADS // 2, 2 * n_t),
        in_specs=[pl.BlockSpec(memory_space=pltpu.SMEM),
                  pl.BlockSpec((None, t_rows, LANES), lambda b, p, t: (b, fwd_blk(t), CB_RET_Q + p)),
                  pl.BlockSpec((None, t_rows, LANES), lambda b, p, t: (b, blk_of(t), CB_RET_K + p)),
                  pl.BlockSpec((None, t_rows, 2 * LANES), lambda b, p, t: (b, blk_of(t), CB_RET_V // 2 + p)),
                  pl.BlockSpec((None, t_rows, 2 * LANES), lambda b, p, t: (b, fwd_blk(t), CB_RET_G // 2 + p)),
                  pl.BlockSpec((t_rows, LANES), lambda b, p, t: (blk_of(t), 0)),
                  pl.BlockSpec((t_rows, LANES), lambda b, p, t: (blk_of(t), 0)),
                  pl.BlockSpec((1, 2 * LANES), lambda b, p, t: (0, p))],
        out_specs=pl.BlockSpec((None, t_rows, 2 * LANES), lambda b, p, t: (b, fwd_blk(t), p)),
        scratch_shapes=[pltpu.VMEM((2, n_t, LANES, LANES), F32),
                        pltpu.VMEM((2, LANES, LANES), F32),
                        pltpu.VMEM((2, LANES, LANES), F32),
                        pltpu.VMEM((2, t_rows, t_rows), F32)],
        compiler_params=_cparams(("parallel", "parallel", "arbitrary")),
        name="retention",
    )(lg, h3, h3, h3, h3, cos_t, sin_t, norm_g.reshape(1, RET_W))


ATT_QB = 128


def _t5_bucket(rel):
    nb = REL_BUCKETS // 2
    max_exact = nb // 2
    sign_off = jnp.where(rel > 0, nb, 0)
    n = jnp.abs(rel)
    nf = jnp.maximum(n, 1).astype(F32)
    large = max_exact + (jnp.log(nf / max_exact) / math.log(REL_MAX_DIST / max_exact)
                         * (nb - max_exact)).astype(jnp.int32)
    large = jnp.minimum(large, nb - 1)
    return sign_off + jnp.where(n < max_exact, n, large)


def _attn_kernel(tbl_ref, q_ref, kp_ref, k_ref, kn_ref, vp_ref, v_ref, vn_ref, qg_ref, kg_ref,
                 bkt_ref, o_ref, lse_ref, *, lq, l_total, head0):
    slot = pl.program_id(2)
    n = pl.program_id(3)
    kw = ATT_QB + 2 * DIL_HALF
    bkt = bkt_ref[...]
    bias = jnp.zeros((ATT_QB, kw), F32)
    for bucket in range(REL_BUCKETS):
        bias = jnp.where(bkt == bucket, tbl_ref[bucket, head0 + slot], bias)
    ii = lax.broadcasted_iota(jnp.int32, (ATT_QB, kw), 0)
    jj = lax.broadcasted_iota(jnp.int32, (ATT_QB, kw), 1)
    in_band = jnp.abs(jj - DIL_HALF - ii) <= DIL_HALF

    q = _rms_rows(q_ref[...].astype(F32), qg_ref[...]) * (LANES ** -0.5)
    kall = jnp.concatenate([kp_ref[...], k_ref[...], kn_ref[...]], axis=0).astype(F32)
    kall = _rms_rows(kall, kg_ref[...]).astype(BF16)
    vall = jnp.concatenate([vp_ref[...], v_ref[...], vn_ref[...]], axis=0).astype(BF16)
    qb16 = q.astype(BF16)
    for qb in range(lq // ATT_QB):
        r0 = qb * ATT_QB
        key_idx = n * lq + r0 - DIL_HALF + jj
        valid = in_band & (key_idx >= 0) & (key_idx < l_total)
        s = _dot_nt(qb16[r0:r0 + ATT_QB], kall[r0:r0 + kw]) + bias
        s = jnp.where(valid, s, NEG_BIG)
        m = jnp.max(s, axis=-1, keepdims=True)
        p = jnp.exp(s - m)
        den = jnp.sum(p, axis=-1, keepdims=True)
        o = _dot(p.astype(BF16), vall[r0:r0 + kw]) / den
        o_ref[r0:r0 + ATT_QB, :] = o
        lse_ref[r0:r0 + ATT_QB, :] = jnp.broadcast_to(m + jnp.log(den), (ATT_QB, LANES))


def dilated_group(hq_all, bsz, rel_bias, q_gain, k_gain, g, lq_max):
    m = hq_all.shape[1]
    s = m // bsz
    dil = DIL_GROUPS[g][1]
    l_total = s // dil
    lq = min(l_total, lq_max)
    n_blk = l_total // lq
    hpb = lq // DIL_HALF
    n_halo = l_total // DIL_HALF
    if dil == 1:
        hr, lead = hq_all.reshape(-1, bsz, 1, s, 3 * DIL_W), g
    else:
        hr = hq_all[g].reshape(bsz, l_total, dil, 3 * DIL_W).transpose(0, 2, 1, 3)[None]
        lead = 0
    cq, ck, cv = 0, DIL_SLOTS, 2 * DIL_SLOTS

    def main(cb):
        return pl.BlockSpec((None, None, None, lq, LANES), lambda b, r, sl, n: (lead, b, r, n, cb + sl))

    def prev(cb):
        return pl.BlockSpec((None, None, None, DIL_HALF, LANES),
                            lambda b, r, sl, n: (lead, b, r, jnp.maximum(n * hpb - 1, 0), cb + sl))

    def nxt(cb):
        return pl.BlockSpec((None, None, None, DIL_HALF, LANES),
                            lambda b, r, sl, n: (lead, b, r, jnp.minimum((n + 1) * hpb, n_halo - 1), cb + sl))

    ii = jnp.arange(ATT_QB)[:, None]
    jj = jnp.arange(ATT_QB + 2 * DIL_HALF)[None, :]
    bkt = _t5_bucket((jj - DIL_HALF - ii) * dil).astype(jnp.int32)
    gain = pl.BlockSpec((1, LANES), lambda b, r, sl, n: (0, 0))
    out_spec = pl.BlockSpec((None, None, lq, LANES), lambda b, r, sl, n: (b, r, n, sl))
    o, lse = pl.pallas_call(
        functools.partial(_attn_kernel, lq=lq, l_total=l_total, head0=g * DIL_SLOTS),
        out_shape=(jax.ShapeDtypeStruct((bsz, dil, l_total, DIL_W), F32),
                   jax.ShapeDtypeStruct((bsz, dil, l_total, DIL_W), F32)),
        grid=(bsz, dil, DIL_SLOTS, n_blk),
        in_specs=[pl.BlockSpec(memory_space=pltpu.SMEM),
                  main(cq), prev(ck), main(ck), nxt(ck), prev(cv), main(cv), nxt(cv),
                  gain, gain,
                  pl.BlockSpec(bkt.shape, lambda b, r, sl, n: (0, 0))],
        out_specs=(out_spec, out_spec),
        compiler_params=_cparams(("parallel", "parallel", "parallel", "arbitrary")),
        name=f"dilated_attn_g{g}",
    )(rel_bias, hr, hr, hr, hr, hr, hr, hr, q_gain.reshape(1, LANES), k_gain.reshape(1, LANES), bkt)
    back = lambda t: t.transpose(0, 2, 1, 3).reshape(m, DIL_W)
    return back(o), back(lse)


def _combine_kernel(o0_ref, o1_ref, o2_ref, l0_ref, l1_ref, l2_ref, y_ref):
    l0, l1, l2 = l0_ref[...], l1_ref[...], l2_ref[...]
    m = jnp.maximum(jnp.maximum(l0, l1), l2)
    w0, w1, w2 = jnp.exp(l0 - m), jnp.exp(l1 - m), jnp.exp(l2 - m)
    y = (w0 * o0_ref[...] + w1 * o1_ref[...] + w2 * o2_ref[...]) / (w0 + w1 + w2)
    y_ref[...] = y.astype(y_ref.dtype)


def dilated_mixer(hq_all, bsz, rel_bias, q_gain, k_gain, lq_max, tm):
    outs = [dilated_group(hq_all, bsz, rel_bias, q_gain, k_gain, g, lq_max)
            for g in range(len(DIL_GROUPS))]
    m = outs[0][0].shape[0]
    spec = pl.BlockSpec((tm, DIL_W), lambda i: (i, 0))
    return pl.pallas_call(
        _combine_kernel,
        out_shape=jax.ShapeDtypeStruct((m, DIL_W), BF16),
        grid=(m // tm,),
        in_specs=[spec] * 6,
        out_specs=spec,
        compiler_params=_cparams(("parallel",)),
        name="dilated_combine",
    )(outs[0][0], outs[1][0], outs[2][0], outs[0][1], outs[1][1], outs[2][1])


def _tiles(bsz, s):
    m = bsz * s
    return dict(tm=min(1024, m), tm_small=min(512, m), hg_t=min(512, s), ret_t=min(512, s),
                lq_max=1024)


def kernel(x, w_in, w_out, w_up, w_down, norm_mix, norm_mlp, hg_lb_fwd, hg_lb_bwd,
           hg_norm, ret_norm, q_norm, k_norm, rel_bias):
    bsz, s, d = x.shape
    m = bsz * s
    tl = _tiles(bsz, s)
    lb_f = jnp.cumsum(jax.nn.softmax(hg_lb_fwd.astype(F32), axis=0), axis=0)
    lb_b = jnp.cumsum(jax.nn.softmax(hg_lb_bwd.astype(F32), axis=0), axis=0)
    lb_f = lb_f - lb_f[0]
    lb_b = lb_b - lb_b[0]
    w_in_b, w_out_b = w_in.astype(BF16), w_out.astype(BF16)
    w_up_b, w_down_b = w_up.astype(BF16), w_down.astype(BF16)

    xf = x.reshape(m, d)
    xn = rmsnorm(xf, norm_mix[0], tl["tm_small"])
    mix_cols = CB_DIL * LANES
    for l in range(DEPTH):
        h3 = in_proj(xn, w_in_b, l, tl["tm"], 1536, 0, mix_cols).reshape(bsz, s, mix_cols)
        hq_all = in_proj_groups(xn, w_in_b, l, tl["tm"], mix_cols)
        ya = hgrn2_mixer(h3, lb_f[l], lb_b[l], hg_norm[l], tl["hg_t"]).reshape(m, HG_W)
        yb = retention_mixer(h3, ret_norm[l], tl["ret_t"]).reshape(m, RET_W)
        yc = dilated_mixer(hq_all, bsz, rel_bias, q_norm[l], k_norm[l], tl["lq_max"], tl["tm_small"])
        xf, hn = out_proj(ya, yb, yc, w_out_b, l, xf, norm_mlp[l], tl["tm_small"])
        last = l == DEPTH - 1
        xf, xn = mlp(hn, w_up_b, w_down_b, l, xf, norm_mix[(l + 1) % DEPTH],
                     tl["tm_small"], 1024, not last)
    return xf.reshape(bsz, s, d)
```

```python
import functools
import math

import numpy as np
import jax
import jax.numpy as jnp
from jax import lax
from jax.experimental import pallas as pl
from jax.experimental.pallas import tpu as pltpu

F32 = jnp.float32
BF16 = jnp.bfloat16
EPS = 1e-6

D_MODEL = 2048
DEPTH = 4
LANES = 128

HG_HEADS = 6
HG_W = HG_HEADS * LANES
RET_HEADS = 6
RET_DK = 64
RET_W = RET_HEADS * LANES
ROPE_BASE = 10000.0
DIL_SLOTS = 4
DIL_GROUPS = ((128, 1), (512, 4), (2048, 16))
DIL_W = DIL_SLOTS * LANES
DIL_HALF = 64
MIX_W = HG_W + RET_W + DIL_W
D_FF = 4 * D_MODEL
REL_BUCKETS = 32
REL_MAX_DIST = 1024

CB_HG_Q, CB_HG_V, CB_HG_ZF, CB_HG_ZB, CB_HG_G = 0, 6, 12, 18, 24
CB_RET_Q, CB_RET_K = 30, 33
CB_RET_V, CB_RET_G = 36, 42
CB_DIL = 48
IN_W = 10752
IN_CB = IN_W // LANES

H_DTYPE = BF16
VMEM_LIMIT = 56 * 1024 * 1024

HG_CHUNK = 128
HG_PAR = 3
LOG2E = 1.4426950408889634
NEG_BIG = -1e30


def _cparams(sem):
    return pltpu.CompilerParams(dimension_semantics=sem, vmem_limit_bytes=VMEM_LIMIT)


def _dot(a, b):
    return jnp.dot(a, b, preferred_element_type=F32)


def _dot_nt(a, b):
    return lax.dot_general(a, b, (((1,), (1,)), ((), ())), preferred_element_type=F32)


def _dot_tn(a, b):
    return lax.dot_general(a, b, (((0,), (0,)), ((), ())), preferred_element_type=F32)


def _rms_rows(x, g):
    return x * lax.rsqrt(jnp.mean(x * x, axis=-1, keepdims=True) + EPS) * g


def _rmsnorm_kernel(x_ref, g_ref, o_ref):
    o_ref[...] = _rms_rows(x_ref[...], g_ref[...]).astype(o_ref.dtype)


def rmsnorm(x, g, tm):
    m, d = x.shape
    return pl.pallas_call(
        _rmsnorm_kernel,
        out_shape=jax.ShapeDtypeStruct((m, d), BF16),
        grid=(m // tm,),
        in_specs=[pl.BlockSpec((tm, d), lambda i: (i, 0)),
                  pl.BlockSpec((1, d), lambda i: (0, 0))],
        out_specs=pl.BlockSpec((tm, d), lambda i: (i, 0)),
        compiler_params=_cparams(("parallel",)),
        name="rmsnorm",
    )(x, g.reshape(1, d))


def _matmul_kernel(x_ref, w_ref, o_ref):
    o_ref[...] = _dot(x_ref[...], w_ref[...]).astype(o_ref.dtype)


def in_proj(xn, w_all, layer, tm, tn, col0, ncols):
    m, d = xn.shape
    j0 = col0 // tn
    return pl.pallas_call(
        _matmul_kernel,
        out_shape=jax.ShapeDtypeStruct((m, ncols), H_DTYPE),
        grid=(m // tm, ncols // tn),
        in_specs=[pl.BlockSpec((tm, d), lambda i, j: (i, 0)),
                  pl.BlockSpec((None, d, tn), lambda i, j: (layer, 0, j0 + j))],
        out_specs=pl.BlockSpec((tm, tn), lambda i, j: (i, j)),
        compiler_params=_cparams(("parallel", "arbitrary")),
        name="in_proj",
    )(xn, w_all)


def in_proj_groups(xn, w_all, layer, tm, col0):
    m, d = xn.shape
    tn = 3 * DIL_W
    j0 = col0 // tn
    return pl.pallas_call(
        _matmul_kernel,
        out_shape=jax.ShapeDtypeStruct((len(DIL_GROUPS), m, tn), F32),
        grid=(m // tm, len(DIL_GROUPS)),
        in_specs=[pl.BlockSpec((tm, d), lambda i, j: (i, 0)),
                  pl.BlockSpec((None, d, tn), lambda i, j: (layer, 0, j0 + j))],
        out_specs=pl.BlockSpec((None, tm, tn), lambda i, j: (j, i, 0)),
        compiler_params=_cparams(("parallel", "arbitrary")),
        name="in_proj_groups",
    )(xn, w_all)


def _out_proj_kernel(ya_ref, yb_ref, yc_ref, w_ref, x_ref, g_ref, xo_ref, hn_ref):
    y = (_dot(ya_ref[...], w_ref[0:HG_W, :])
         + _dot(yb_ref[...], w_ref[HG_W:HG_W + RET_W, :])
         + _dot(yc_ref[...], w_ref[HG_W + RET_W:MIX_W, :]))
    xo = x_ref[...] + y
    xo_ref[...] = xo
    hn_ref[...] = _rms_rows(xo, g_ref[...]).astype(hn_ref.dtype)


def out_proj(ya, yb, yc, w_all, layer, x, g, tm):
    m, d = x.shape
    row = lambda i: (i, 0)
    return pl.pallas_call(
        _out_proj_kernel,
        out_shape=(jax.ShapeDtypeStruct((m, d), F32), jax.ShapeDtypeStruct((m, d), BF16)),
        grid=(m // tm,),
        in_specs=[pl.BlockSpec((tm, HG_W), row), pl.BlockSpec((tm, RET_W), row),
                  pl.BlockSpec((tm, DIL_W), row),
                  pl.BlockSpec((None, MIX_W, d), lambda i: (layer, 0, 0)),
                  pl.BlockSpec((tm, d), row),
                  pl.BlockSpec((1, d), lambda i: (0, 0))],
        out_specs=(pl.BlockSpec((tm, d), row), pl.BlockSpec((tm, d), row)),
        compiler_params=_cparams(("parallel",)),
        name="out_proj",
    )(ya, yb, yc, w_all, x, g.reshape(1, d))


def _mlp_kernel(h_ref, wu_ref, wd_ref, x_ref, g_ref, xo_ref, *rest, nj, emit_norm):
    j = pl.program_id(1)

    @pl.when(j == 0)
    def _():
        xo_ref[...] = x_ref[...]

    a = jnp.maximum(_dot(h_ref[...], wu_ref[...]), 0.0)
    a = (a * a).astype(BF16)
    xo_ref[...] += _dot(a, wd_ref[...])

    if emit_norm:
        xn_ref = rest[0]

        @pl.when(j == nj - 1)
        def _():
            xn_ref[...] = _rms_rows(xo_ref[...], g_ref[...]).astype(xn_ref.dtype)


def mlp(hn, wu_all, wd_all, layer, x, g, tm, tf, emit_norm):
    m, d = x.shape
    f = wu_all.shape[2]
    nj = f // tf
    row = lambda i, j: (i, 0)
    out_shape = [jax.ShapeDtypeStruct((m, d), F32)]
    out_specs = [pl.BlockSpec((tm, d), row)]
    if emit_norm:
        out_shape.append(jax.ShapeDtypeStruct((m, d), BF16))
        out_specs.append(pl.BlockSpec((tm, d), row))
    res = pl.pallas_call(
        functools.partial(_mlp_kernel, nj=nj, emit_norm=emit_norm),
        out_shape=tuple(out_shape),
        grid=(m // tm, nj),
        in_specs=[pl.BlockSpec((tm, d), row),
                  pl.BlockSpec((None, d, tf), lambda i, j: (layer, 0, j)),
                  pl.BlockSpec((None, tf, d), lambda i, j: (layer, j, 0)),
                  pl.BlockSpec((tm, d), row),
                  pl.BlockSpec((1, d), lambda i, j: (0, 0))],
        out_specs=tuple(out_specs),
        compiler_params=_cparams(("parallel", "arbitrary")),
        name="mlp",
    )(hn, wu_all, wd_all, x, g.reshape(1, d))
    return res if emit_norm else (res[0], None)


def _hg_constants(c):
    i = np.arange(c)[:, None]
    j = np.arange(c)[None, :]
    x = i ^ j
    lvl = np.where(x == 0, 0, np.floor(np.log2(np.maximum(x, 1))).astype(np.int64) + 1)
    tri_f = (j <= i).astype(np.float32)
    tri_r = (j >= i).astype(np.float32)
    lv_f = np.where(i >= j, lvl, -1).astype(np.int32)
    lv_r = np.where(i <= j, lvl, -1).astype(np.int32)
    return (jnp.asarray(tri_f, BF16), jnp.asarray(tri_r, BF16),
            jnp.asarray(lv_f), jnp.asarray(lv_r))


def _hg_ref_rows(b, m, reverse, rowid):
    c = b.shape[0]
    g = 2 * m
    r = m if reverse else m - 1
    if g >= 8:
        b3 = b.reshape(c // g, g, LANES)
        return jnp.broadcast_to(b3[:, r:r + 1, :], b3.shape).reshape(c, LANES)
    pos = rowid % g
    out = b
    for p in range(g):
        off = r - p
        if off == 0:
            continue
        out = jnp.where(pos == p, pltpu.roll(b, (-off) % c, 0), out)
    return out


def _neg_abs(x):
    bits = lax.bitcast_convert_type(x, jnp.uint32) | jnp.uint32(0x80000000)
    return lax.bitcast_convert_type(bits, F32)


def _hg_chunk(q, k, v, lf2, tri, lv, rowid, st, reverse):
    c = q.shape[0]
    hi = lf2.astype(BF16)
    lo = (lf2 - hi.astype(F32)).astype(BF16)
    b = _dot(tri, hi) + _dot(tri, lo)
    vb = v.astype(BF16)
    qb, kb = q.astype(BF16), k.astype(BF16)
    a = jnp.where(lv == 0, _dot_nt(qb, kb), 0.0)
    for ell in range(1, int(math.log2(c)) + 1):
        m = 2 ** (ell - 1)
        fac = jnp.exp2(_neg_abs(b - _hg_ref_rows(b, m, reverse, rowid)))
        a = jnp.where(lv == ell, _dot_nt((q * fac).astype(BF16), (k * fac).astype(BF16)), a)
    out = _dot(a.astype(BF16), vb) + _dot_nt((q * jnp.exp2(b)).astype(BF16), st.astype(BF16))
    b_edge = b[0:1, :] if reverse else b[c - 1:c, :]
    kk = (k * jnp.exp2(b_edge - b)).astype(BF16)
    st_new = st * jnp.exp2(b_edge) + _dot_tn(vb, kk)
    return out, st_new


def _hgrn2_kernel(q_ref, v_ref, zf_ref, zb_ref, g_ref, lbf_ref, lbb_ref, ng_ref,
                  trif_ref, trir_ref, lvf_ref, lvr_ref, o_ref, st_ref, ob_ref, *, t_rows, n_t):
    t = pl.program_id(2)
    c = HG_CHUNK
    n_c = t_rows // c
    rowid = lax.broadcasted_iota(jnp.int32, (c, LANES), 0)

    def sweep(z_ref, lb_ref, tri_ref, lv_ref, reverse, blk):
        tri = tri_ref[...]
        lv = lv_ref[...]

        def body(ci, carry):
            cc = (n_c - 1 - ci) if reverse else ci
            rows = pl.ds(pl.multiple_of(cc * c, c), c)
            srows = pl.ds(pl.multiple_of(blk * t_rows + cc * c, c), c)
            for hh in range(HG_PAR):
                cols = slice(hh * LANES, (hh + 1) * LANES)
                lb = lb_ref[:, cols]
                q = q_ref[rows, cols].astype(F32)
                v = v_ref[rows, cols].astype(F32)
                f = lb + (1.0 - lb) * jax.nn.sigmoid(z_ref[rows, cols].astype(F32))
                out, st_new = _hg_chunk(q, 1.0 - f, v, jnp.log(f) * LOG2E, tri, lv, rowid,
                                        st_ref[hh], reverse)
                st_ref[hh] = st_new
                if reverse:
                    ob_ref[srows, cols] = out
                else:
                    o = out + ob_ref[srows, cols]
                    gate = g_ref[rows, cols].astype(F32)
                    y = _rms_rows(o, ng_ref[:, cols]) * (gate * jax.nn.sigmoid(gate))
                    o_ref[rows, cols] = y.astype(o_ref.dtype)
            return carry

        lax.fori_loop(0, n_c, body, 0, unroll=True)

    @pl.when((t == 0) | (t == n_t))
    def _():
        st_ref[...] = jnp.zeros_like(st_ref)

    @pl.when(t < n_t)
    def _():
        sweep(zb_ref, lbb_ref, trir_ref, lvr_ref, True, n_t - 1 - t)

    @pl.when(t >= n_t)
    def _():
        sweep(zf_ref, lbf_ref, trif_ref, lvf_ref, False, t - n_t)


def hgrn2_mixer(h3, lb_f, lb_b, norm_g, t_rows):
    bsz, s, _ = h3.shape
    n_t = s // t_rows
    w = HG_PAR * LANES
    tri_f, tri_r, lv_f, lv_r = _hg_constants(HG_CHUNK)

    def both(cb):
        return pl.BlockSpec((None, t_rows, w),
                            lambda b, h, t: (b, jnp.where(t < n_t, n_t - 1 - t, t - n_t), cb // HG_PAR + h))

    def bwd_only(cb):
        return pl.BlockSpec((None, t_rows, w),
                            lambda b, h, t: (b, jnp.maximum(n_t - 1 - t, 0), cb // HG_PAR + h))

    def fwd_only(cb):
        return pl.BlockSpec((None, t_rows, w),
                            lambda b, h, t: (b, jnp.maximum(t - n_t, 0), cb // HG_PAR + h))

    head_row = pl.BlockSpec((1, w), lambda b, h, t: (0, h))
    const = pl.BlockSpec((HG_CHUNK, HG_CHUNK), lambda b, h, t: (0, 0))
    return pl.pallas_call(
        functools.partial(_hgrn2_kernel, t_rows=t_rows, n_t=n_t),
        out_shape=jax.ShapeDtypeStruct((bsz, s, HG_W), BF16),
        grid=(bsz, HG_HEADS // HG_PAR, 2 * n_t),
        in_specs=[both(CB_HG_Q), both(CB_HG_V), fwd_only(CB_HG_ZF), bwd_only(CB_HG_ZB),
                  fwd_only(CB_HG_G), head_row, head_row, head_row, const, const, const, const],
        out_specs=pl.BlockSpec((None, t_rows, w), lambda b, h, t: (b, jnp.maximum(t - n_t, 0), h)),
        scratch_shapes=[pltpu.VMEM((HG_PAR, LANES, LANES), F32), pltpu.VMEM((s, w), F32)],
        compiler_params=_cparams(("parallel", "parallel", "arbitrary")),
        name="hgrn2",
    )(h3, h3, h3, h3, h3, lb_f.reshape(1, HG_W), lb_b.reshape(1, HG_W),
      norm_g.reshape(1, HG_W), tri_f, tri_r, lv_f, lv_r)


def _rope_tables(s):
    half = RET_DK // 2
    inv = ROPE_BASE ** (-jnp.arange(half, dtype=F32) / half)
    ang = jnp.arange(s, dtype=F32)[:, None] * inv[None, :]
    cos, sin = jnp.cos(ang), jnp.sin(ang)
    cos_t = jnp.concatenate([cos, cos, cos, cos], axis=-1)
    sin_t = jnp.concatenate([-sin, sin, -sin, sin], axis=-1)
    return cos_t, sin_t


def _rope_pair(x, cos_t, sin_t, lane):
    half = RET_DK // 2
    swapped = jnp.where(lane % RET_DK < half, pltpu.roll(x, LANES - half, 1), pltpu.roll(x, half, 1))
    return x * cos_t + swapped * sin_t


def _retention_kernel(lg_ref, q_ref, k_ref, v_ref, g_ref, cos_ref, sin_ref, ng_ref, o_ref,
                      sb_ref, sbc_ref, sf_ref, d_ref, *, t_rows, n_t):
    pair = pl.program_id(1)
    t = pl.program_id(2)
    tr = t_rows
    lane = lax.broadcasted_iota(jnp.int32, (tr, LANES), 1)
    row = lax.broadcasted_iota(jnp.int32, (tr, LANES), 0).astype(F32)
    cos_t = cos_ref[...]
    sin_t = sin_ref[...]
    k = _rope_pair(k_ref[...].astype(F32), cos_t, sin_t, lane) * (RET_DK ** -0.5)

    @pl.when(t == 0)
    def _():
        sbc_ref[...] = jnp.zeros_like(sbc_ref)

    @pl.when(t == n_t)
    def _():
        sf_ref[...] = jnp.zeros_like(sf_ref)
        ri = lax.broadcasted_iota(jnp.int32, (tr, tr), 0)
        ci = lax.broadcasted_iota(jnp.int32, (tr, tr), 1)
        rel = (ri - ci).astype(F32)
        for hh in range(2):
            lg_f = lg_ref[0, 2 * pair + hh]
            lg_b = lg_ref[1, 2 * pair + hh]
            d_ref[hh] = (jnp.exp(lg_f * jnp.maximum(rel, 0.0) + lg_b * jnp.maximum(-rel, 0.0))
                         + jnp.where(rel == 0.0, 1.0, 0.0))

    @pl.when(t < n_t)
    def _():
        blk = n_t - 1 - t
        for hh in range(2):
            lg_b = lg_ref[1, 2 * pair + hh]
            km = jnp.where(lane // RET_DK == hh, k, 0.0)
            vb = v_ref[:, hh * LANES:(hh + 1) * LANES].astype(BF16)
            cur = sbc_ref[hh]
            sb_ref[hh, blk] = cur
            zeta = jnp.exp(lg_b * row)
            sbc_ref[hh] = cur * jnp.exp(lg_b * tr) + _dot_tn((km * zeta).astype(BF16), vb)

    @pl.when(t >= n_t)
    def _():
        blk = t - n_t
        q = _rope_pair(q_ref[...].astype(F32), cos_t, sin_t, lane)
        qb = q.astype(BF16)
        for hh in range(2):
            lg_f = lg_ref[0, 2 * pair + hh]
            lg_b = lg_ref[1, 2 * pair + hh]
            km = jnp.where(lane // RET_DK == hh, k, 0.0)
            vb = v_ref[:, hh * LANES:(hh + 1) * LANES].astype(BF16)
            p = (_dot_nt(qb, km.astype(BF16)) * d_ref[hh]).astype(BF16)
            sf = sf_ref[hh]
            o = (_dot(p, vb)
                 + _dot((q * jnp.exp(lg_f * (row + 1.0))).astype(BF16), sf.astype(BF16))
                 + _dot((q * jnp.exp(lg_b * (tr - row))).astype(BF16), sb_ref[hh, blk].astype(BF16)))
            zeta = jnp.exp(lg_f * (tr - 1.0 - row))
            sf_ref[hh] = sf * jnp.exp(lg_f * tr) + _dot_tn((km * zeta).astype(BF16), vb)
            mu = jnp.mean(o, axis=-1, keepdims=True)
            oc = o - mu
            var = jnp.mean(oc * oc, axis=-1, keepdims=True)
            gate = g_ref[:, hh * LANES:(hh + 1) * LANES].astype(F32)
            y = oc * lax.rsqrt(var + EPS) * ng_ref[:, hh * LANES:(hh + 1) * LANES]
            o_ref[:, hh * LANES:(hh + 1) * LANES] = (y * (gate * jax.nn.sigmoid(gate))).astype(o_ref.dtype)


def retention_mixer(h3, norm_g, t_rows):
    bsz, s, _ = h3.shape
    n_t = s // t_rows
    cos_t, sin_t = _rope_tables(s)
    hidx = jnp.arange(RET_HEADS, dtype=F32)
    lg_f = jnp.log1p(-jnp.exp2(-5.0 - hidx))
    lg = jnp.stack([lg_f, lg_f[::-1]], axis=0)
    blk_of = lambda t: jnp.where(t < n_t, n_t - 1 - t, t - n_t)
    fwd_blk = lambda t: jnp.maximum(t - n_t, 0)
    return pl.pallas_call(
        functools.partial(_retention_kernel, t_rows=t_rows, n_t=n_t),
        out_shape=jax.ShapeDtypeStruct((bsz, s, RET_W), BF16),
        grid=(bsz, RET_HEADS // 2, 2 * n_t),
        in_specs=[pl.BlockSpec(memory_space=pltpu.SMEM),
                  pl.BlockSpec((None, t_rows, LANES), lambda b, p, t: (b, fwd_blk(t), CB_RET_Q + p)),
                  pl.BlockSpec((None, t_rows, LANES), lambda b, p, t: (b, blk_of(t), CB_RET_K + p)),
                  pl.BlockSpec((None, t_rows, 2 * LANES), lambda b, p, t: (b, blk_of(t), CB_RET_V // 2 + p)),
                  pl.BlockSpec((None, t_rows, 2 * LANES), lambda b, p, t: (b, fwd_blk(t), CB_RET_G // 2 + p)),
                  pl.BlockSpec((t_rows, LANES), lambda b, p, t: (blk_of(t), 0)),
                  pl.BlockSpec((t_rows, LANES), lambda b, p, t: (blk_of(t), 0)),
                  pl.BlockSpec((1, 2 * LANES), lambda b, p, t: (0, p))],
        out_specs=pl.BlockSpec((None, t_rows, 2 * LANES), lambda b, p, t: (b, fwd_blk(t), p)),
        scratch_shapes=[pltpu.VMEM((2, n_t, LANES, LANES), F32),
                        pltpu.VMEM((2, LANES, LANES), F32),
                        pltpu.VMEM((2, LANES, LANES), F32),
                        pltpu.VMEM((2, t_rows, t_rows), F32)],
        compiler_params=_cparams(("parallel", "parallel", "arbitrary")),
        name="retention",
    )(lg, h3, h3, h3, h3, cos_t, sin_t, norm_g.reshape(1, RET_W))


ATT_QB = 128


def _t5_bucket(rel):
    nb = REL_BUCKETS // 2
    max_exact = nb // 2
    sign_off = jnp.where(rel > 0, nb, 0)
    n = jnp.abs(rel)
    nf = jnp.maximum(n, 1).astype(F32)
    large = max_exact + (jnp.log(nf / max_exact) / math.log(REL_MAX_DIST / max_exact)
                         * (nb - max_exact)).astype(jnp.int32)
    large = jnp.minimum(large, nb - 1)
    return sign_off + jnp.where(n < max_exact, n, large)


def _dilated_kernel(tbl_ref, *refs, t_tok, s_len):
    n_g = len(DIL_GROUPS)
    grp = [refs[7 * g:7 * g + 7] for g in range(n_g)]
    qg_ref, kg_ref, bkt_ref, y_ref = refs[7 * n_g:7 * n_g + 4]
    og = refs[7 * n_g + 4:7 * n_g + 4 + n_g]
    lg = refs[7 * n_g + 4 + n_g:]
    slot = pl.program_id(1)
    t = pl.program_id(2)
    kw = ATT_QB + 2 * DIL_HALF
    ii = lax.broadcasted_iota(jnp.int32, (ATT_QB, kw), 0)
    jj = lax.broadcasted_iota(jnp.int32, (ATT_QB, kw), 1)
    in_band = jnp.abs(jj - DIL_HALF - ii) <= DIL_HALF
    qg = qg_ref[...] * (LANES ** -0.5)
    kg = kg_ref[...]

    for g, (_, dil) in enumerate(DIL_GROUPS):
        q_ref, kp_ref, k_ref, kn_ref, vp_ref, v_ref, vn_ref = grp[g]
        nq = t_tok // dil
        l_total = s_len // dil
        bkt = bkt_ref[g]
        bias = jnp.zeros((ATT_QB, kw), F32)
        for bucket in range(REL_BUCKETS):
            bias = jnp.where(bkt == bucket, tbl_ref[bucket, g * DIL_SLOTS + slot], bias)

        def residue(r, dil=dil, nq=nq, l_total=l_total, bias=bias, g=g, q_ref=q_ref, kp_ref=kp_ref,
                    k_ref=k_ref, kn_ref=kn_ref, vp_ref=vp_ref, v_ref=v_ref, vn_ref=vn_ref):
            rows = lambda ref, n: ref[pl.ds(r, n, stride=dil), :]
            q = _rms_rows(rows(q_ref, nq), qg).astype(BF16)
            kall = jnp.concatenate([rows(kp_ref, DIL_HALF), rows(k_ref, nq), rows(kn_ref, DIL_HALF)], axis=0)
            kall = _rms_rows(kall, kg).astype(BF16)
            vall = jnp.concatenate([rows(vp_ref, DIL_HALF), rows(v_ref, nq), rows(vn_ref, DIL_HALF)],
                                   axis=0).astype(BF16)
            for qb in range(nq // ATT_QB):
                r0 = qb * ATT_QB
                key_idx = t * nq + r0 - DIL_HALF + jj
                valid = in_band & (key_idx >= 0) & (key_idx < l_total)
                s = _dot_nt(q[r0:r0 + ATT_QB], kall[r0:r0 + kw]) + bias
                s = jnp.where(valid, s, NEG_BIG)
                m = jnp.max(s, axis=-1, keepdims=True)
                p = jnp.exp(s - m)
                den = jnp.sum(p, axis=-1, keepdims=True)
                o = _dot(p.astype(BF16), vall[r0:r0 + kw]) / den
                dst = pl.ds(r + r0 * dil, ATT_QB, stride=dil)
                og[g][dst, :] = o
                lg[g][dst, :] = jnp.broadcast_to(m + jnp.log(den), (ATT_QB, LANES))

        if dil <= 4:
            for r in range(dil):
                residue(r)
        else:
            def body(r, carry, residue=residue):
                residue(r)
                return carry
            lax.fori_loop(0, dil, body, 0)

    l0, l1, l2 = lg[0][...], lg[1][...], lg[2][...]
    m = jnp.maximum(jnp.maximum(l0, l1), l2)
    w0, w1, w2 = jnp.exp(l0 - m), jnp.exp(l1 - m), jnp.exp(l2 - m)
    y = (w0 * og[0][...] + w1 * og[1][...] + w2 * og[2][...]) / (w0 + w1 + w2)
    y_ref[...] = y.astype(y_ref.dtype)


def dilated_mixer(hq_all, bsz, rel_bias, q_gain, k_gain, t_tok):
    n_g, m, _ = hq_all.shape
    s = m // bsz
    nb = s // t_tok
    cq, ck, cv = 0, DIL_SLOTS, 2 * DIL_SLOTS
    in_specs = [pl.BlockSpec(memory_space=pltpu.SMEM)]
    for g, (_, dil) in enumerate(DIL_GROUPS):
        halo = DIL_HALF * dil
        hpb = t_tok // halo
        n_halo = m // halo

        def main(cb, g=g):
            return pl.BlockSpec((None, t_tok, LANES), lambda b, sl, t: (g, b * nb + t, cb + sl))

        def prev(cb, g=g, halo=halo, hpb=hpb):
            return pl.BlockSpec((None, halo, LANES),
                                lambda b, sl, t: (g, jnp.maximum((b * nb + t) * hpb - 1, 0), cb + sl))

        def nxt(cb, g=g, halo=halo, hpb=hpb, n_halo=n_halo):
            return pl.BlockSpec((None, halo, LANES),
                                lambda b, sl, t: (g, jnp.minimum((b * nb + t + 1) * hpb, n_halo - 1), cb + sl))

        in_specs += [main(cq), prev(ck), main(ck), nxt(ck), prev(cv), main(cv), nxt(cv)]
    ii = jnp.arange(ATT_QB)[:, None]
    jj = jnp.arange(ATT_QB + 2 * DIL_HALF)[None, :]
    bkt = jnp.stack([_t5_bucket((jj - DIL_HALF - ii) * dil) for _, dil in DIL_GROUPS]).astype(jnp.int32)
    gain = pl.BlockSpec((1, LANES), lambda b, sl, t: (0, 0))
    in_specs += [gain, gain, pl.BlockSpec(bkt.shape, lambda b, sl, t: (0, 0, 0))]
    return pl.pallas_call(
        functools.partial(_dilated_kernel, t_tok=t_tok, s_len=s),
        out_shape=jax.ShapeDtypeStruct((m, DIL_W), BF16),
        grid=(bsz, DIL_SLOTS, nb),
        in_specs=in_specs,
        out_specs=pl.BlockSpec((t_tok, LANES), lambda b, sl, t: (b * nb + t, sl)),
        scratch_shapes=[pltpu.VMEM((t_tok, LANES), F32)] * (2 * n_g),
        compiler_params=_cparams(("parallel", "parallel", "arbitrary")),
        name="dilated_attn",
    )(rel_bias, *([hq_all] * (7 * n_g)), q_gain.reshape(1, LANES), k_gain.reshape(1, LANES), bkt)


def _tiles(bsz, s):
    m = bsz * s
    return dict(tm=min(1024, m), tm_small=min(512, m), hg_t=min(512, s), ret_t=min(512, s),
                att_t=min(2048, s))


def kernel(x, w_in, w_out, w_up, w_down, norm_mix, norm_mlp, hg_lb_fwd, hg_lb_bwd,
           hg_norm, ret_norm, q_norm, k_norm, rel_bias):
    bsz, s, d = x.shape
    m = bsz * s
    tl = _tiles(bsz, s)
    lb_f = jnp.cumsum(jax.nn.softmax(hg_lb_fwd.astype(F32), axis=0), axis=0)
    lb_b = jnp.cumsum(jax.nn.softmax(hg_lb_bwd.astype(F32), axis=0), axis=0)
    lb_f = lb_f - lb_f[0]
    lb_b = lb_b - lb_b[0]
    w_in_b, w_out_b = w_in.astype(BF16), w_out.astype(BF16)
    w_up_b, w_down_b = w_up.astype(BF16), w_down.astype(BF16)

    xf = x.reshape(m, d)
    xn = rmsnorm(xf, norm_mix[0], tl["tm_small"])
    mix_cols = CB_DIL * LANES
    for l in range(DEPTH):
        h3 = in_proj(xn, w_in_b, l, tl["tm"], 1536, 0, mix_cols).reshape(bsz, s, mix_cols)
        hq_all = in_proj_groups(xn, w_in_b, l, tl["tm"], mix_cols)
        ya = hgrn2_mixer(h3, lb_f[l], lb_b[l], hg_norm[l], tl["hg_t"]).reshape(m, HG_W)
        yb = retention_mixer(h3, ret_norm[l], tl["ret_t"]).reshape(m, RET_W)
        yc = dilated_mixer(hq_all, bsz, rel_bias, q_norm[l], k_norm[l], tl["att_t"])
        xf, hn = out_proj(ya, yb, yc, w_out_b, l, xf, norm_mlp[l], tl["tm_small"])
        last = l == DEPTH - 1
        xf, xn = mlp(hn, w_up_b, w_down_b, l, xf, norm_mix[(l + 1) % DEPTH],
                     tl["tm_small"], 1024, not last)
    return xf.reshape(bsz, s, d)
```

```python
import functools
import math

import numpy as np
import jax
import jax.numpy as jnp
from jax import lax
from jax.experimental import pallas as pl
from jax.experimental.pallas import tpu as pltpu

F32 = jnp.float32
BF16 = jnp.bfloat16
EPS = 1e-6

D_MODEL = 2048
DEPTH = 4
LANES = 128

HG_HEADS = 6
HG_W = HG_HEADS * LANES
RET_HEADS = 6
RET_DK = 64
RET_W = RET_HEADS * LANES
ROPE_BASE = 10000.0
DIL_SLOTS = 4
DIL_GROUPS = ((128, 1), (512, 4), (2048, 16))
DIL_W = DIL_SLOTS * LANES
DIL_HALF = 64
MIX_W = HG_W + RET_W + DIL_W
D_FF = 4 * D_MODEL
REL_BUCKETS = 32
REL_MAX_DIST = 1024

CB_HG_Q, CB_HG_V, CB_HG_ZF, CB_HG_ZB, CB_HG_G = 0, 6, 12, 18, 24
CB_RET_Q, CB_RET_K = 30, 33
CB_RET_V, CB_RET_G = 36, 42
CB_DIL = 48
IN_W = 10752
IN_CB = IN_W // LANES

H_DTYPE = BF16
VMEM_LIMIT = 56 * 1024 * 1024

HG_CHUNK = 128
HG_PAR = 3
LOG2E = 1.4426950408889634
NEG_BIG = -1e30


def _cparams(sem):
    return pltpu.CompilerParams(dimension_semantics=sem, vmem_limit_bytes=VMEM_LIMIT)


def _dot(a, b):
    return jnp.dot(a, b, preferred_element_type=F32)


def _dot_nt(a, b):
    return lax.dot_general(a, b, (((1,), (1,)), ((), ())), preferred_element_type=F32)


def _dot_tn(a, b):
    return lax.dot_general(a, b, (((0,), (0,)), ((), ())), preferred_element_type=F32)


def _rms_rows(x, g):
    return x * lax.rsqrt(jnp.mean(x * x, axis=-1, keepdims=True) + EPS) * g


def _rmsnorm_kernel(x_ref, g_ref, o_ref):
    o_ref[...] = _rms_rows(x_ref[...], g_ref[...]).astype(o_ref.dtype)


def rmsnorm(x, g, tm):
    m, d = x.shape
    return pl.pallas_call(
        _rmsnorm_kernel,
        out_shape=jax.ShapeDtypeStruct((m, d), BF16),
        grid=(m // tm,),
        in_specs=[pl.BlockSpec((tm, d), lambda i: (i, 0)),
                  pl.BlockSpec((1, d), lambda i: (0, 0))],
        out_specs=pl.BlockSpec((tm, d), lambda i: (i, 0)),
        compiler_params=_cparams(("parallel",)),
        name="rmsnorm",
    )(x, g.reshape(1, d))


def _matmul_kernel(x_ref, w_ref, o_ref):
    o_ref[...] = _dot(x_ref[...], w_ref[...]).astype(o_ref.dtype)


def in_proj(xn, w_all, layer, tm, tn, col0, ncols):
    m, d = xn.shape
    j0 = col0 // tn
    return pl.pallas_call(
        _matmul_kernel,
        out_shape=jax.ShapeDtypeStruct((m, ncols), H_DTYPE),
        grid=(m // tm, ncols // tn),
        in_specs=[pl.BlockSpec((tm, d), lambda i, j: (i, 0)),
                  pl.BlockSpec((None, d, tn), lambda i, j: (layer, 0, j0 + j))],
        out_specs=pl.BlockSpec((tm, tn), lambda i, j: (i, j)),
        compiler_params=_cparams(("parallel", "arbitrary")),
        name="in_proj",
    )(xn, w_all)


def in_proj_groups(xn, w_all, layer, tm, col0):
    m, d = xn.shape
    tn = 3 * DIL_W
    j0 = col0 // tn
    return pl.pallas_call(
        _matmul_kernel,
        out_shape=jax.ShapeDtypeStruct((len(DIL_GROUPS), m, tn), F32),
        grid=(m // tm, len(DIL_GROUPS)),
        in_specs=[pl.BlockSpec((tm, d), lambda i, j: (i, 0)),
                  pl.BlockSpec((None, d, tn), lambda i, j: (layer, 0, j0 + j))],
        out_specs=pl.BlockSpec((None, tm, tn), lambda i, j: (j, i, 0)),
        compiler_params=_cparams(("parallel", "arbitrary")),
        name="in_proj_groups",
    )(xn, w_all)


def _out_proj_kernel(ya_ref, yb_ref, yc_ref, w_ref, x_ref, g_ref, xo_ref, hn_ref):
    y = (_dot(ya_ref[...], w_ref[0:HG_W, :])
         + _dot(yb_ref[...], w_ref[HG_W:HG_W + RET_W, :])
         + _dot(yc_ref[...], w_ref[HG_W + RET_W:MIX_W, :]))
    xo = x_ref[...] + y
    xo_ref[...] = xo
    hn_ref[...] = _rms_rows(xo, g_ref[...]).astype(hn_ref.dtype)


def out_proj(ya, yb, yc, w_all, layer, x, g, tm):
    m, d = x.shape
    row = lambda i: (i, 0)
    return pl.pallas_call(
        _out_proj_kernel,
        out_shape=(jax.ShapeDtypeStruct((m, d), F32), jax.ShapeDtypeStruct((m, d), BF16)),
        grid=(m // tm,),
        in_specs=[pl.BlockSpec((tm, HG_W), row), pl.BlockSpec((tm, RET_W), row),
                  pl.BlockSpec((tm, DIL_W), row),
                  pl.BlockSpec((None, MIX_W, d), lambda i: (layer, 0, 0)),
                  pl.BlockSpec((tm, d), row),
                  pl.BlockSpec((1, d), lambda i: (0, 0))],
        out_specs=(pl.BlockSpec((tm, d), row), pl.BlockSpec((tm, d), row)),
        compiler_params=_cparams(("parallel",)),
        name="out_proj",
    )(ya, yb, yc, w_all, x, g.reshape(1, d))


def _mlp_kernel(h_ref, wu_ref, wd_ref, x_ref, g_ref, xo_ref, *rest, nj, emit_norm):
    j = pl.program_id(1)

    @pl.when(j == 0)
    def _():
        xo_ref[...] = x_ref[...]

    a = jnp.maximum(_dot(h_ref[...], wu_ref[...]), 0.0)
    a = (a * a).astype(BF16)
    xo_ref[...] += _dot(a, wd_ref[...])

    if emit_norm:
        xn_ref = rest[0]

        @pl.when(j == nj - 1)
        def _():
            xn_ref[...] = _rms_rows(xo_ref[...], g_ref[...]).astype(xn_ref.dtype)


def mlp(hn, wu_all, wd_all, layer, x, g, tm, tf, emit_norm):
    m, d = x.shape
    f = wu_all.shape[2]
    nj = f // tf
    row = lambda i, j: (i, 0)
    out_shape = [jax.ShapeDtypeStruct((m, d), F32)]
    out_specs = [pl.BlockSpec((tm, d), row)]
    if emit_norm:
        out_shape.append(jax.ShapeDtypeStruct((m, d), BF16))
        out_specs.append(pl.BlockSpec((tm, d), row))
    res = pl.pallas_call(
        functools.partial(_mlp_kernel, nj=nj, emit_norm=emit_norm),
        out_shape=tuple(out_shape),
        grid=(m // tm, nj),
        in_specs=[pl.BlockSpec((tm, d), row),
                  pl.BlockSpec((None, d, tf), lambda i, j: (layer, 0, j)),
                  pl.BlockSpec((None, tf, d), lambda i, j: (layer, j, 0)),
                  pl.BlockSpec((tm, d), row),
                  pl.BlockSpec((1, d), lambda i, j: (0, 0))],
        out_specs=tuple(out_specs),
        compiler_params=_cparams(("parallel", "arbitrary")),
        name="mlp",
    )(hn, wu_all, wd_all, x, g.reshape(1, d))
    return res if emit_norm else (res[0], None)


def _hg_constants(c):
    i = np.arange(c)[:, None]
    j = np.arange(c)[None, :]
    x = i ^ j
    lvl = np.where(x == 0, 0, np.floor(np.log2(np.maximum(x, 1))).astype(np.int64) + 1)
    tri_f = (j <= i).astype(np.float32)
    tri_r = (j >= i).astype(np.float32)
    lv_f = np.where(i >= j, lvl, -1).astype(np.int32)
    lv_r = np.where(i <= j, lvl, -1).astype(np.int32)
    return (jnp.asarray(tri_f, BF16), jnp.asarray(tri_r, BF16),
            jnp.asarray(lv_f), jnp.asarray(lv_r))


def _hg_ref_rows(b, m, reverse, rowid):
    c = b.shape[0]
    g = 2 * m
    r = m if reverse else m - 1
    if g >= 8:
        b3 = b.reshape(c // g, g, LANES)
        return jnp.broadcast_to(b3[:, r:r + 1, :], b3.shape).reshape(c, LANES)
    pos = rowid % g
    out = b
    for p in range(g):
        off = r - p
        if off == 0:
            continue
        out = jnp.where(pos == p, pltpu.roll(b, (-off) % c, 0), out)
    return out


def _neg_abs(x):
    bits = lax.bitcast_convert_type(x, jnp.uint32) | jnp.uint32(0x80000000)
    return lax.bitcast_convert_type(bits, F32)


def _hg_chunk(q, k, v, lf2, tri, lv, rowid, st, reverse):
    c = q.shape[0]
    hi = lf2.astype(BF16)
    lo = (lf2 - hi.astype(F32)).astype(BF16)
    b = _dot(tri, hi) + _dot(tri, lo)
    vb = v.astype(BF16)
    qb, kb = q.astype(BF16), k.astype(BF16)
    a = jnp.where(lv == 0, _dot_nt(qb, kb), 0.0)
    for ell in range(1, int(math.log2(c)) + 1):
        m = 2 ** (ell - 1)
        fac = jnp.exp2(_neg_abs(b - _hg_ref_rows(b, m, reverse, rowid)))
        a = jnp.where(lv == ell, _dot_nt((q * fac).astype(BF16), (k * fac).astype(BF16)), a)
    out = _dot(a.astype(BF16), vb) + _dot_nt((q * jnp.exp2(b)).astype(BF16), st.astype(BF16))
    b_edge = b[0:1, :] if reverse else b[c - 1:c, :]
    kk = (k * jnp.exp2(b_edge - b)).astype(BF16)
    st_new = st * jnp.exp2(b_edge) + _dot_tn(vb, kk)
    return out, st_new


def _hgrn2_kernel(q_ref, v_ref, zf_ref, zb_ref, g_ref, lbf_ref, lbb_ref, ng_ref,
                  trif_ref, trir_ref, lvf_ref, lvr_ref, o_ref, st_ref, ob_ref, *, t_rows, n_t):
    t = pl.program_id(2)
    c = HG_CHUNK
    n_c = t_rows // c
    rowid = lax.broadcasted_iota(jnp.int32, (c, LANES), 0)

    def sweep(z_ref, lb_ref, tri_ref, lv_ref, reverse, blk):
        tri = tri_ref[...]
        lv = lv_ref[...]

        def body(ci, carry):
            cc = (n_c - 1 - ci) if reverse else ci
            rows = pl.ds(pl.multiple_of(cc * c, c), c)
            srows = pl.ds(pl.multiple_of(blk * t_rows + cc * c, c), c)
            for hh in range(HG_PAR):
                cols = slice(hh * LANES, (hh + 1) * LANES)
                lb = lb_ref[:, cols]
                q = q_ref[rows, cols].astype(F32)
                v = v_ref[rows, cols].astype(F32)
                f = lb + (1.0 - lb) * jax.nn.sigmoid(z_ref[rows, cols].astype(F32))
                out, st_new = _hg_chunk(q, 1.0 - f, v, jnp.log(f) * LOG2E, tri, lv, rowid,
                                        st_ref[hh], reverse)
                st_ref[hh] = st_new
                if reverse:
                    ob_ref[srows, cols] = out
                else:
                    o = out + ob_ref[srows, cols]
                    gate = g_ref[rows, cols].astype(F32)
                    y = _rms_rows(o, ng_ref[:, cols]) * (gate * jax.nn.sigmoid(gate))
                    o_ref[rows, cols] = y.astype(o_ref.dtype)
            return carry

        lax.fori_loop(0, n_c, body, 0, unroll=True)

    @pl.when((t == 0) | (t == n_t))
    def _():
        st_ref[...] = jnp.zeros_like(st_ref)

    @pl.when(t < n_t)
    def _():
        sweep(zb_ref, lbb_ref, trir_ref, lvr_ref, True, n_t - 1 - t)

    @pl.when(t >= n_t)
    def _():
        sweep(zf_ref, lbf_ref, trif_ref, lvf_ref, False, t - n_t)


def hgrn2_mixer(h3, lb_f, lb_b, norm_g, t_rows):
    bsz, s, _ = h3.shape
    n_t = s // t_rows
    w = HG_PAR * LANES
    tri_f, tri_r, lv_f, lv_r = _hg_constants(HG_CHUNK)

    def both(cb):
        return pl.BlockSpec((None, t_rows, w),
                            lambda b, h, t: (b, jnp.where(t < n_t, n_t - 1 - t, t - n_t), cb // HG_PAR + h))

    def bwd_only(cb):
        return pl.BlockSpec((None, t_rows, w),
                            lambda b, h, t: (b, jnp.maximum(n_t - 1 - t, 0), cb // HG_PAR + h))

    def fwd_only(cb):
        return pl.BlockSpec((None, t_rows, w),
                            lambda b, h, t: (b, jnp.maximum(t - n_t, 0), cb // HG_PAR + h))

    head_row = pl.BlockSpec((1, w), lambda b, h, t: (0, h))
    const = pl.BlockSpec((HG_CHUNK, HG_CHUNK), lambda b, h, t: (0, 0))
    return pl.pallas_call(
        functools.partial(_hgrn2_kernel, t_rows=t_rows, n_t=n_t),
        out_shape=jax.ShapeDtypeStruct((bsz, s, HG_W), BF16),
        grid=(bsz, HG_HEADS // HG_PAR, 2 * n_t),
        in_specs=[both(CB_HG_Q), both(CB_HG_V), fwd_only(CB_HG_ZF), bwd_only(CB_HG_ZB),
                  fwd_only(CB_HG_G), head_row, head_row, head_row, const, const, const, const],
        out_specs=pl.BlockSpec((None, t_rows, w), lambda b, h, t: (b, jnp.maximum(t - n_t, 0), h)),
        scratch_shapes=[pltpu.VMEM((HG_PAR, LANES, LANES), F32), pltpu.VMEM((s, w), F32)],
        compiler_params=_cparams(("parallel", "parallel", "arbitrary")),
        name="hgrn2",
    )(h3, h3, h3, h3, h3, lb_f.reshape(1, HG_W), lb_b.reshape(1, HG_W),
      norm_g.reshape(1, HG_W), tri_f, tri_r, lv_f, lv_r)


def _rope_tables(s):
    half = RET_DK // 2
    inv = ROPE_BASE ** (-jnp.arange(half, dtype=F32) / half)
    ang = jnp.arange(s, dtype=F32)[:, None] * inv[None, :]
    cos, sin = jnp.cos(ang), jnp.sin(ang)
    cos_t = jnp.concatenate([cos, cos, cos, cos], axis=-1)
    sin_t = jnp.concatenate([-sin, sin, -sin, sin], axis=-1)
    return cos_t, sin_t


def _rope_pair(x, cos_t, sin_t, lane):
    half = RET_DK // 2
    swapped = jnp.where(lane % RET_DK < half, pltpu.roll(x, LANES - half, 1), pltpu.roll(x, half, 1))
    return x * cos_t + swapped * sin_t


def _retention_kernel(lg_ref, q_ref, k_ref, v_ref, g_ref, cos_ref, sin_ref, ng_ref, o_ref,
                      sb_ref, sbc_ref, sf_ref, d_ref, *, t_rows, n_t):
    pair = pl.program_id(1)
    t = pl.program_id(2)
    tr = t_rows
    lane = lax.broadcasted_iota(jnp.int32, (tr, LANES), 1)
    row = lax.broadcasted_iota(jnp.int32, (tr, LANES), 0).astype(F32)
    cos_t = cos_ref[...]
    sin_t = sin_ref[...]
    k = _rope_pair(k_ref[...].astype(F32), cos_t, sin_t, lane) * (RET_DK ** -0.5)

    @pl.when(t == 0)
    def _():
        sbc_ref[...] = jnp.zeros_like(sbc_ref)

    @pl.when(t == n_t)
    def _():
        sf_ref[...] = jnp.zeros_like(sf_ref)
        ri = lax.broadcasted_iota(jnp.int32, (tr, tr), 0)
        ci = lax.broadcasted_iota(jnp.int32, (tr, tr), 1)
        rel = (ri - ci).astype(F32)
        for hh in range(2):
            lg_f = lg_ref[0, 2 * pair + hh]
            lg_b = lg_ref[1, 2 * pair + hh]
            d_ref[hh] = (jnp.exp(lg_f * jnp.maximum(rel, 0.0) + lg_b * jnp.maximum(-rel, 0.0))
                         + jnp.where(rel == 0.0, 1.0, 0.0))

    @pl.when(t < n_t)
    def _():
        blk = n_t - 1 - t
        for hh in range(2):
            lg_b = lg_ref[1, 2 * pair + hh]
            km = jnp.where(lane // RET_DK == hh, k, 0.0)
            vb = v_ref[:, hh * LANES:(hh + 1) * LANES].astype(BF16)
            cur = sbc_ref[hh]
            sb_ref[hh, blk] = cur
            zeta = jnp.exp(lg_b * row)
            sbc_ref[hh] = cur * jnp.exp(lg_b * tr) + _dot_tn((km * zeta).astype(BF16), vb)

    @pl.when(t >= n_t)
    def _():
        blk = t - n_t
        q = _rope_pair(q_ref[...].astype(F32), cos_t, sin_t, lane)
        qb = q.astype(BF16)
        for hh in range(2):
            lg_f = lg_ref[0, 2 * pair + hh]
            lg_b = lg_ref[1, 2 * pair + hh]
            km = jnp.where(lane // RET_DK == hh, k, 0.0)
            vb = v_ref[:, hh * LANES:(hh + 1) * LANES].astype(BF16)
            p = (_dot_nt(qb, km.astype(BF16)) * d_ref[hh]).astype(BF16)
            sf = sf_ref[hh]
            o = (_dot(p, vb)
                 + _dot((q * jnp.exp(lg_f * (row + 1.0))).astype(BF16), sf.astype(BF16))
                 + _dot((q * jnp.exp(lg_b * (tr - row))).astype(BF16), sb_ref[hh, blk].astype(BF16)))
            zeta = jnp.exp(lg_f * (tr - 1.0 - row))
            sf_ref[hh] = sf * jnp.exp(lg_f * tr) + _dot_tn((km * zeta).astype(BF16), vb)
            mu = jnp.mean(o, axis=-1, keepdims=True)
            oc = o - mu
            var = jnp.mean(oc * oc, axis=-1, keepdims=True)
            gate = g_ref[:, hh * LANES:(hh + 1) * LANES].astype(F32)
            y = oc * lax.rsqrt(var + EPS) * ng_ref[:, hh * LANES:(hh + 1) * LANES]
            o_ref[:, hh * LANES:(hh + 1) * LANES] = (y * (gate * jax.nn.sigmoid(gate))).astype(o_ref.dtype)


def retention_mixer(h3, norm_g, t_rows):
    bsz, s, _ = h3.shape
    n_t = s // t_rows
    cos_t, sin_t = _rope_tables(s)
    hidx = jnp.arange(RET_HEADS, dtype=F32)
    lg_f = jnp.log1p(-jnp.exp2(-5.0 - hidx))
    lg = jnp.stack([lg_f, lg_f[::-1]], axis=0)
    blk_of = lambda t: jnp.where(t < n_t, n_t - 1 - t, t - n_t)
    fwd_blk = lambda t: jnp.maximum(t - n_t, 0)
    return pl.pallas_call(
        functools.partial(_retention_kernel, t_rows=t_rows, n_t=n_t),
        out_shape=jax.ShapeDtypeStruct((bsz, s, RET_W), BF16),
        grid=(bsz, RET_HEADS // 2, 2 * n_t),
        in_specs=[pl.BlockSpec(memory_space=pltpu.SMEM),
                  pl.BlockSpec((None, t_rows, LANES), lambda b, p, t: (b, fwd_blk(t), CB_RET_Q + p)),
                  pl.BlockSpec((None, t_rows, LANES), lambda b, p, t: (b, blk_of(t), CB_RET_K + p)),
                  pl.BlockSpec((None, t_rows, 2 * LANES), lambda b, p, t: (b, blk_of(t), CB_RET_V // 2 + p)),
                  pl.BlockSpec((None, t_rows, 2 * LANES), lambda b, p, t: (b, fwd_blk(t), CB_RET_G // 2 + p)),
                  pl.BlockSpec((t_rows, LANES), lambda b, p, t: (blk_of(t), 0)),
                  pl.BlockSpec((t_rows, LANES), lambda b, p, t: (blk_of(t), 0)),
                  pl.BlockSpec((1, 2 * LANES), lambda b, p, t: (0, p))],
        out_specs=pl.BlockSpec((None, t_rows, 2 * LANES), lambda b, p, t: (b, fwd_blk(t), p)),
        scratch_shapes=[pltpu.VMEM((2, n_t, LANES, LANES), F32),
                        pltpu.VMEM((2, LANES, LANES), F32),
                        pltpu.VMEM((2, LANES, LANES), F32),
                        pltpu.VMEM((2, t_rows, t_rows), F32)],
        compiler_params=_cparams(("parallel", "parallel", "arbitrary")),
        name="retention",
    )(lg, h3, h3, h3, h3, cos_t, sin_t, norm_g.reshape(1, RET_W))


ATT_QB = 128


def _t5_bucket(rel):
    nb = REL_BUCKETS // 2
    max_exact = nb // 2
    sign_off = jnp.where(rel > 0, nb, 0)
    n = jnp.abs(rel)
    nf = jnp.maximum(n, 1).astype(F32)
    large = max_exact + (jnp.log(nf / max_exact) / math.log(REL_MAX_DIST / max_exact)
                         * (nb - max_exact)).astype(jnp.int32)
    large = jnp.minimum(large, nb - 1)
    return sign_off + jnp.where(n < max_exact, n, large)


def _dilated_kernel(tbl_ref, *refs, t_tok, s_len):
    n_g = len(DIL_GROUPS)
    grp = [refs[7 * g:7 * g + 7] for g in range(n_g)]
    qg_ref, kg_ref, bkt_ref, y_ref = refs[7 * n_g:7 * n_g + 4]
    og = refs[7 * n_g + 4:7 * n_g + 4 + n_g]
    lg = refs[7 * n_g + 4 + n_g:]
    slot = pl.program_id(1)
    t = pl.program_id(2)
    kw = ATT_QB + 2 * DIL_HALF
    ii = lax.broadcasted_iota(jnp.int32, (ATT_QB, kw), 0)
    jj = lax.broadcasted_iota(jnp.int32, (ATT_QB, kw), 1)
    in_band = jnp.abs(jj - DIL_HALF - ii) <= DIL_HALF
    qg = qg_ref[...] * (LANES ** -0.5)
    kg = kg_ref[...]

    for g, (_, dil) in enumerate(DIL_GROUPS):
        q_ref, kp_ref, k_ref, kn_ref, vp_ref, v_ref, vn_ref = grp[g]
        nq = t_tok // dil
        l_total = s_len // dil
        bkt = bkt_ref[g]
        bias = jnp.zeros((ATT_QB, kw), F32)
        for bucket in range(REL_BUCKETS):
            bias = jnp.where(bkt == bucket, tbl_ref[bucket, g * DIL_SLOTS + slot], bias)

        def residue(r, dil=dil, nq=nq, l_total=l_total, bias=bias, g=g, q_ref=q_ref, kp_ref=kp_ref,
                    k_ref=k_ref, kn_ref=kn_ref, vp_ref=vp_ref, v_ref=v_ref, vn_ref=vn_ref):
            rows = lambda ref, n: ref[pl.ds(r, n, stride=dil), :]
            q = _rms_rows(rows(q_ref, nq), qg).astype(BF16)
            kall = jnp.concatenate([rows(kp_ref, DIL_HALF), rows(k_ref, nq), rows(kn_ref, DIL_HALF)], axis=0)
            kall = _rms_rows(kall, kg).astype(BF16)
            vall = jnp.concatenate([rows(vp_ref, DIL_HALF), rows(v_ref, nq), rows(vn_ref, DIL_HALF)],
                                   axis=0).astype(BF16)
            for qb in range(nq // ATT_QB):
                r0 = qb * ATT_QB
                key_idx = t * nq + r0 - DIL_HALF + jj
                valid = in_band & (key_idx >= 0) & (key_idx < l_total)
                s = _dot_nt(q[r0:r0 + ATT_QB], kall[r0:r0 + kw]) + bias
                s = jnp.where(valid, s, NEG_BIG)
                m = jnp.max(s, axis=-1, keepdims=True)
                p = jnp.exp(s - m)
                den = jnp.sum(p, axis=-1, keepdims=True)
                o = _dot(p.astype(BF16), vall[r0:r0 + kw]) / den
                dst = pl.ds(r + r0 * dil, ATT_QB, stride=dil)
                og[g][dst, :] = o
                lg[g][dst, :] = jnp.broadcast_to(m + jnp.log(den), (ATT_QB, LANES))

        if dil <= 4:
            for r in range(dil):
                residue(r)
        else:
            def body(r, carry, residue=residue):
                residue(r)
                return carry
            lax.fori_loop(0, dil, body, 0, unroll=4)

    l0, l1, l2 = lg[0][...], lg[1][...], lg[2][...]
    m = jnp.maximum(jnp.maximum(l0, l1), l2)
    w0, w1, w2 = jnp.exp(l0 - m), jnp.exp(l1 - m), jnp.exp(l2 - m)
    y = (w0 * og[0][...] + w1 * og[1][...] + w2 * og[2][...]) / (w0 + w1 + w2)
    y_ref[...] = y.astype(y_ref.dtype)


def dilated_mixer(hq_all, bsz, rel_bias, q_gain, k_gain, t_tok):
    n_g, m, _ = hq_all.shape
    s = m // bsz
    nb = s // t_tok
    cq, ck, cv = 0, DIL_SLOTS, 2 * DIL_SLOTS
    in_specs = [pl.BlockSpec(memory_space=pltpu.SMEM)]
    for g, (_, dil) in enumerate(DIL_GROUPS):
        halo = DIL_HALF * dil
        hpb = t_tok // halo
        n_halo = m // halo

        def main(cb, g=g):
            return pl.BlockSpec((None, t_tok, LANES), lambda b, sl, t: (g, b * nb + t, cb + sl))

        def prev(cb, g=g, halo=halo, hpb=hpb):
            return pl.BlockSpec((None, halo, LANES),
                                lambda b, sl, t: (g, jnp.maximum((b * nb + t) * hpb - 1, 0), cb + sl))

        def nxt(cb, g=g, halo=halo, hpb=hpb, n_halo=n_halo):
            return pl.BlockSpec((None, halo, LANES),
                                lambda b, sl, t: (g, jnp.minimum((b * nb + t + 1) * hpb, n_halo - 1), cb + sl))

        in_specs += [main(cq), prev(ck), main(ck), nxt(ck), prev(cv), main(cv), nxt(cv)]
    ii = jnp.arange(ATT_QB)[:, None]
    jj = jnp.arange(ATT_QB + 2 * DIL_HALF)[None, :]
    bkt = jnp.stack([_t5_bucket((jj - DIL_HALF - ii) * dil) for _, dil in DIL_GROUPS]).astype(jnp.int32)
    gain = pl.BlockSpec((1, LANES), lambda b, sl, t: (0, 0))
    in_specs += [gain, gain, pl.BlockSpec(bkt.shape, lambda b, sl, t: (0, 0, 0))]
    return pl.pallas_call(
        functools.partial(_dilated_kernel, t_tok=t_tok, s_len=s),
        out_shape=jax.ShapeDtypeStruct((m, DIL_W), BF16),
        grid=(bsz, DIL_SLOTS, nb),
        in_specs=in_specs,
        out_specs=pl.BlockSpec((t_tok, LANES), lambda b, sl, t: (b * nb + t, sl)),
        scratch_shapes=[pltpu.VMEM((t_tok, LANES), F32)] * (2 * n_g),
        compiler_params=_cparams(("parallel", "parallel", "arbitrary")),
        name="dilated_attn",
    )(rel_bias, *([hq_all] * (7 * n_g)), q_gain.reshape(1, LANES), k_gain.reshape(1, LANES), bkt)


def _tiles(bsz, s):
    m = bsz * s
    return dict(tm=min(1024, m), tm_small=min(512, m), hg_t=min(512, s), ret_t=min(512, s),
                att_t=min(2048, s))


def kernel(x, w_in, w_out, w_up, w_down, norm_mix, norm_mlp, hg_lb_fwd, hg_lb_bwd,
           hg_norm, ret_norm, q_norm, k_norm, rel_bias):
    bsz, s, d = x.shape
    m = bsz * s
    tl = _tiles(bsz, s)
    lb_f = jnp.cumsum(jax.nn.softmax(hg_lb_fwd.astype(F32), axis=0), axis=0)
    lb_b = jnp.cumsum(jax.nn.softmax(hg_lb_bwd.astype(F32), axis=0), axis=0)
    lb_f = lb_f - lb_f[0]
    lb_b = lb_b - lb_b[0]
    w_in_b, w_out_b = w_in.astype(BF16), w_out.astype(BF16)
    w_up_b, w_down_b = w_up.astype(BF16), w_down.astype(BF16)

    xf = x.reshape(m, d)
    xn = rmsnorm(xf, norm_mix[0], tl["tm_small"])
    mix_cols = CB_DIL * LANES
    for l in range(DEPTH):
        h3 = in_proj(xn, w_in_b, l, tl["tm"], 1536, 0, mix_cols).reshape(bsz, s, mix_cols)
        hq_all = in_proj_groups(xn, w_in_b, l, tl["tm"], mix_cols)
        ya = hgrn2_mixer(h3, lb_f[l], lb_b[l], hg_norm[l], tl["hg_t"]).reshape(m, HG_W)
        yb = retention_mixer(h3, ret_norm[l], tl["ret_t"]).reshape(m, RET_W)
        yc = dilated_mixer(hq_all, bsz, rel_bias, q_norm[l], k_norm[l], tl["att_t"])
        xf, hn = out_proj(ya, yb, yc, w_out_b, l, xf, norm_mlp[l], tl["tm_small"])
        last = l == DEPTH - 1
        xf, xn = mlp(hn, w_up_b, w_down_b, l, xf, norm_mix[(l + 1) % DEPTH],
                     tl["tm_small"], 1024, not last)
    return xf.reshape(bsz, s, d)
```

```python
import functools
import math

import numpy as np
import jax
import jax.numpy as jnp
from jax import lax
from jax.experimental import pallas as pl
from jax.experimental.pallas import tpu as pltpu

F32 = jnp.float32
BF16 = jnp.bfloat16
EPS = 1e-6

D_MODEL = 2048
DEPTH = 4
LANES = 128

HG_HEADS = 6
HG_W = HG_HEADS * LANES
RET_HEADS = 6
RET_DK = 64
RET_W = RET_HEADS * LANES
ROPE_BASE = 10000.0
DIL_SLOTS = 4
DIL_GROUPS = ((128, 1), (512, 4), (2048, 16))
DIL_W = DIL_SLOTS * LANES
DIL_HALF = 64
MIX_W = HG_W + RET_W + DIL_W
D_FF = 4 * D_MODEL
REL_BUCKETS = 32
REL_MAX_DIST = 1024

CB_HG_Q, CB_HG_V, CB_HG_ZF, CB_HG_ZB, CB_HG_G = 0, 6, 12, 18, 24
CB_RET_Q, CB_RET_K = 30, 33
CB_RET_V, CB_RET_G = 36, 42
CB_DIL = 48
IN_W = 10752
IN_CB = IN_W // LANES

H_DTYPE = BF16
VMEM_LIMIT = 56 * 1024 * 1024

HG_CHUNK = 128
HG_PAR = 3
LOG2E = 1.4426950408889634
NEG_BIG = -1e30


def _cparams(sem):
    return pltpu.CompilerParams(dimension_semantics=sem, vmem_limit_bytes=VMEM_LIMIT)


def _dot(a, b):
    return jnp.dot(a, b, preferred_element_type=F32)


def _dot_nt(a, b):
    return lax.dot_general(a, b, (((1,), (1,)), ((), ())), preferred_element_type=F32)


def _dot_tn(a, b):
    return lax.dot_general(a, b, (((0,), (0,)), ((), ())), preferred_element_type=F32)


def _rms_rows(x, g):
    return x * lax.rsqrt(jnp.mean(x * x, axis=-1, keepdims=True) + EPS) * g


def _rmsnorm_kernel(x_ref, g_ref, o_ref):
    o_ref[...] = _rms_rows(x_ref[...], g_ref[...]).astype(o_ref.dtype)


def rmsnorm(x, g, tm):
    m, d = x.shape
    return pl.pallas_call(
        _rmsnorm_kernel,
        out_shape=jax.ShapeDtypeStruct((m, d), BF16),
        grid=(m // tm,),
        in_specs=[pl.BlockSpec((tm, d), lambda i: (i, 0)),
                  pl.BlockSpec((1, d), lambda i: (0, 0))],
        out_specs=pl.BlockSpec((tm, d), lambda i: (i, 0)),
        compiler_params=_cparams(("parallel",)),
        name="rmsnorm",
    )(x, g.reshape(1, d))


def _matmul_kernel(x_ref, w_ref, o_ref, wb_ref):
    @pl.when(pl.program_id(1) == 0)
    def _():
        wb_ref[...] = w_ref[...].astype(BF16)

    o_ref[...] = _dot(x_ref[...], wb_ref[...]).astype(o_ref.dtype)


def in_proj(xn, w_all, layer, tm, tn, col0, ncols):
    m, d = xn.shape
    j0 = col0 // tn
    return pl.pallas_call(
        _matmul_kernel,
        out_shape=jax.ShapeDtypeStruct((m, ncols), H_DTYPE),
        grid=(ncols // tn, m // tm),
        in_specs=[pl.BlockSpec((tm, d), lambda j, i: (i, 0)),
                  pl.BlockSpec((None, d, tn), lambda j, i: (layer, 0, j0 + j))],
        out_specs=pl.BlockSpec((tm, tn), lambda j, i: (i, j)),
        scratch_shapes=[pltpu.VMEM((d, tn), BF16)],
        compiler_params=_cparams(("parallel", "arbitrary")),
        name="in_proj",
    )(xn, w_all)


def in_proj_groups(xn, w_all, layer, tm, col0):
    m, d = xn.shape
    tn = 3 * DIL_W
    j0 = col0 // tn
    return pl.pallas_call(
        _matmul_kernel,
        out_shape=jax.ShapeDtypeStruct((len(DIL_GROUPS), m, tn), F32),
        grid=(len(DIL_GROUPS), m // tm),
        in_specs=[pl.BlockSpec((tm, d), lambda j, i: (i, 0)),
                  pl.BlockSpec((None, d, tn), lambda j, i: (layer, 0, j0 + j))],
        out_specs=pl.BlockSpec((None, tm, tn), lambda j, i: (j, i, 0)),
        scratch_shapes=[pltpu.VMEM((d, tn), BF16)],
        compiler_params=_cparams(("parallel", "arbitrary")),
        name="in_proj_groups",
    )(xn, w_all)


def _out_proj_kernel(ya_ref, yb_ref, yc_ref, w_ref, x_ref, g_ref, xo_ref, hn_ref):
    y = (_dot(ya_ref[...], w_ref[0:HG_W, :])
         + _dot(yb_ref[...], w_ref[HG_W:HG_W + RET_W, :])
         + _dot(yc_ref[...], w_ref[HG_W + RET_W:MIX_W, :]))
    xo = x_ref[...] + y
    xo_ref[...] = xo
    hn_ref[...] = _rms_rows(xo, g_ref[...]).astype(hn_ref.dtype)


def out_proj(ya, yb, yc, w_all, layer, x, g, tm):
    m, d = x.shape
    row = lambda i: (i, 0)
    return pl.pallas_call(
        _out_proj_kernel,
        out_shape=(jax.ShapeDtypeStruct((m, d), F32), jax.ShapeDtypeStruct((m, d), BF16)),
        grid=(m // tm,),
        in_specs=[pl.BlockSpec((tm, HG_W), row), pl.BlockSpec((tm, RET_W), row),
                  pl.BlockSpec((tm, DIL_W), row),
                  pl.BlockSpec((None, MIX_W, d), lambda i: (layer, 0, 0)),
                  pl.BlockSpec((tm, d), row),
                  pl.BlockSpec((1, d), lambda i: (0, 0))],
        out_specs=(pl.BlockSpec((tm, d), row), pl.BlockSpec((tm, d), row)),
        compiler_params=_cparams(("parallel",)),
        name="out_proj",
    )(ya, yb, yc, w_all, x, g.reshape(1, d))


def _mlp_kernel(h_ref, wu_ref, wd_ref, x_ref, g_ref, xo_ref, *rest, nj, emit_norm):
    j = pl.program_id(1)

    @pl.when(j == 0)
    def _():
        xo_ref[...] = x_ref[...]

    a = jnp.maximum(_dot(h_ref[...], wu_ref[...]), 0.0)
    a = (a * a).astype(BF16)
    xo_ref[...] += _dot(a, wd_ref[...])

    if emit_norm:
        xn_ref = rest[0]

        @pl.when(j == nj - 1)
        def _():
            xn_ref[...] = _rms_rows(xo_ref[...], g_ref[...]).astype(xn_ref.dtype)


def mlp(hn, wu_all, wd_all, layer, x, g, tm, tf, emit_norm):
    m, d = x.shape
    f = wu_all.shape[2]
    nj = f // tf
    row = lambda i, j: (i, 0)
    out_shape = [jax.ShapeDtypeStruct((m, d), F32)]
    out_specs = [pl.BlockSpec((tm, d), row)]
    if emit_norm:
        out_shape.append(jax.ShapeDtypeStruct((m, d), BF16))
        out_specs.append(pl.BlockSpec((tm, d), row))
    res = pl.pallas_call(
        functools.partial(_mlp_kernel, nj=nj, emit_norm=emit_norm),
        out_shape=tuple(out_shape),
        grid=(m // tm, nj),
        in_specs=[pl.BlockSpec((tm, d), row),
                  pl.BlockSpec((None, d, tf), lambda i, j: (layer, 0, j)),
                  pl.BlockSpec((None, tf, d), lambda i, j: (layer, j, 0)),
                  pl.BlockSpec((tm, d), row),
                  pl.BlockSpec((1, d), lambda i, j: (0, 0))],
        out_specs=tuple(out_specs),
        compiler_params=_cparams(("parallel", "arbitrary")),
        name="mlp",
    )(hn, wu_all, wd_all, x, g.reshape(1, d))
    return res if emit_norm else (res[0], None)


def _hg_constants(c):
    i = np.arange(c)[:, None]
    j = np.arange(c)[None, :]
    x = i ^ j
    lvl = np.where(x == 0, 0, np.floor(np.log2(np.maximum(x, 1))).astype(np.int64) + 1)
    tri_f = (j <= i).astype(np.float32)
    tri_r = (j >= i).astype(np.float32)
    lv_f = np.where(i >= j, lvl, -1).astype(np.int32)
    lv_r = np.where(i <= j, lvl, -1).astype(np.int32)
    return (jnp.asarray(tri_f, BF16), jnp.asarray(tri_r, BF16),
            jnp.asarray(lv_f), jnp.asarray(lv_r))


def _hg_ref_rows(b, m, reverse, rowid):
    c = b.shape[0]
    g = 2 * m
    r = m if reverse else m - 1
    if g >= 8:
        b3 = b.reshape(c // g, g, LANES)
        return jnp.broadcast_to(b3[:, r:r + 1, :], b3.shape).reshape(c, LANES)
    pos = rowid % g
    out = b
    for p in range(g):
        off = r - p
        if off == 0:
            continue
        out = jnp.where(pos == p, pltpu.roll(b, (-off) % c, 0), out)
    return out


def _neg_abs(x):
    bits = lax.bitcast_convert_type(x, jnp.uint32) | jnp.uint32(0x80000000)
    return lax.bitcast_convert_type(bits, F32)


def _hg_chunk(q, k, v, lf2, tri, lv, rowid, st, reverse):
    c = q.shape[0]
    hi = lf2.astype(BF16)
    lo = (lf2 - hi.astype(F32)).astype(BF16)
    b = _dot(tri, hi) + _dot(tri, lo)
    vb = v.astype(BF16)
    qb, kb = q.astype(BF16), k.astype(BF16)
    a = jnp.where(lv == 0, _dot_nt(qb, kb), 0.0)
    for ell in range(1, int(math.log2(c)) + 1):
        m = 2 ** (ell - 1)
        fac = jnp.exp2(_neg_abs(b - _hg_ref_rows(b, m, reverse, rowid)))
        a = jnp.where(lv == ell, _dot_nt((q * fac).astype(BF16), (k * fac).astype(BF16)), a)
    out = _dot(a.astype(BF16), vb) + _dot_nt((q * jnp.exp2(b)).astype(BF16), st.astype(BF16))
    b_edge = b[0:1, :] if reverse else b[c - 1:c, :]
    kk = (k * jnp.exp2(b_edge - b)).astype(BF16)
    st_new = st * jnp.exp2(b_edge) + _dot_tn(vb, kk)
    return out, st_new


def _hgrn2_kernel(q_ref, v_ref, zf_ref, zb_ref, g_ref, lbf_ref, lbb_ref, ng_ref,
                  trif_ref, trir_ref, lvf_ref, lvr_ref, o_ref, st_ref, ob_ref, *, t_rows, n_t):
    t = pl.program_id(2)
    c = HG_CHUNK
    n_c = t_rows // c
    rowid = lax.broadcasted_iota(jnp.int32, (c, LANES), 0)

    def sweep(z_ref, lb_ref, tri_ref, lv_ref, reverse, blk):
        tri = tri_ref[...]
        lv = lv_ref[...]

        def body(ci, carry):
            cc = (n_c - 1 - ci) if reverse else ci
            rows = pl.ds(pl.multiple_of(cc * c, c), c)
            srows = pl.ds(pl.multiple_of(blk * t_rows + cc * c, c), c)
            for hh in range(HG_PAR):
                cols = slice(hh * LANES, (hh + 1) * LANES)
                lb = lb_ref[:, cols]
                q = q_ref[rows, cols].astype(F32)
                v = v_ref[rows, cols].astype(F32)
                f = lb + (1.0 - lb) * jax.nn.sigmoid(z_ref[rows, cols].astype(F32))
                out, st_new = _hg_chunk(q, 1.0 - f, v, jnp.log(f) * LOG2E, tri, lv, rowid,
                                        st_ref[hh], reverse)
                st_ref[hh] = st_new
                if reverse:
                    ob_ref[srows, cols] = out
                else:
                    o = out + ob_ref[srows, cols]
                    gate = g_ref[rows, cols].astype(F32)
                    y = _rms_rows(o, ng_ref[:, cols]) * (gate * jax.nn.sigmoid(gate))
                    o_ref[rows, cols] = y.astype(o_ref.dtype)
            return carry

        lax.fori_loop(0, n_c, body, 0, unroll=True)

    @pl.when((t == 0) | (t == n_t))
    def _():
        st_ref[...] = jnp.zeros_like(st_ref)

    @pl.when(t < n_t)
    def _():
        sweep(zb_ref, lbb_ref, trir_ref, lvr_ref, True, n_t - 1 - t)

    @pl.when(t >= n_t)
    def _():
        sweep(zf_ref, lbf_ref, trif_ref, lvf_ref, False, t - n_t)


def hgrn2_mixer(h3, lb_f, lb_b, norm_g, t_rows):
    bsz, s, _ = h3.shape
    n_t = s // t_rows
    w = HG_PAR * LANES
    tri_f, tri_r, lv_f, lv_r = _hg_constants(HG_CHUNK)

    def both(cb):
        return pl.BlockSpec((None, t_rows, w),
                            lambda b, h, t: (b, jnp.where(t < n_t, n_t - 1 - t, t - n_t), cb // HG_PAR + h))

    def bwd_only(cb):
        return pl.BlockSpec((None, t_rows, w),
                            lambda b, h, t: (b, jnp.maximum(n_t - 1 - t, 0), cb // HG_PAR + h))

    def fwd_only(cb):
        return pl.BlockSpec((None, t_rows, w),
                            lambda b, h, t: (b, jnp.maximum(t - n_t, 0), cb // HG_PAR + h))

    head_row = pl.BlockSpec((1, w), lambda b, h, t: (0, h))
    const = pl.BlockSpec((HG_CHUNK, HG_CHUNK), lambda b, h, t: (0, 0))
    return pl.pallas_call(
        functools.partial(_hgrn2_kernel, t_rows=t_rows, n_t=n_t),
        out_shape=jax.ShapeDtypeStruct((bsz, s, HG_W), BF16),
        grid=(bsz, HG_HEADS // HG_PAR, 2 * n_t),
        in_specs=[both(CB_HG_Q), both(CB_HG_V), fwd_only(CB_HG_ZF), bwd_only(CB_HG_ZB),
                  fwd_only(CB_HG_G), head_row, head_row, head_row, const, const, const, const],
        out_specs=pl.BlockSpec((None, t_rows, w), lambda b, h, t: (b, jnp.maximum(t - n_t, 0), h)),
        scratch_shapes=[pltpu.VMEM((HG_PAR, LANES, LANES), F32), pltpu.VMEM((s, w), F32)],
        compiler_params=_cparams(("parallel", "parallel", "arbitrary")),
        name="hgrn2",
    )(h3, h3, h3, h3, h3, lb_f.reshape(1, HG_W), lb_b.reshape(1, HG_W),
      norm_g.reshape(1, HG_W), tri_f, tri_r, lv_f, lv_r)


def _rope_tables(s):
    half = RET_DK // 2
    inv = ROPE_BASE ** (-jnp.arange(half, dtype=F32) / half)
    ang = jnp.arange(s, dtype=F32)[:, None] * inv[None, :]
    cos, sin = jnp.cos(ang), jnp.sin(ang)
    cos_t = jnp.concatenate([cos, cos, cos, cos], axis=-1)
    sin_t = jnp.concatenate([-sin, sin, -sin, sin], axis=-1)
    return cos_t, sin_t


def _rope_pair(x, cos_t, sin_t, lane):
    half = RET_DK // 2
    swapped = jnp.where(lane % RET_DK < half, pltpu.roll(x, LANES - half, 1), pltpu.roll(x, half, 1))
    return x * cos_t + swapped * sin_t


def _retention_kernel(lg_ref, q_ref, k_ref, v_ref, g_ref, cos_ref, sin_ref, ng_ref, o_ref,
                      sb_ref, sbc_ref, sf_ref, d_ref, *, t_rows, n_t):
    pair = pl.program_id(1)
    t = pl.program_id(2)
    tr = t_rows
    lane = lax.broadcasted_iota(jnp.int32, (tr, LANES), 1)
    row = lax.broadcasted_iota(jnp.int32, (tr, LANES), 0).astype(F32)
    cos_t = cos_ref[...]
    sin_t = sin_ref[...]
    k = _rope_pair(k_ref[...].astype(F32), cos_t, sin_t, lane) * (RET_DK ** -0.5)

    @pl.when(t == 0)
    def _():
        sbc_ref[...] = jnp.zeros_like(sbc_ref)

    @pl.when(t == n_t)
    def _():
        sf_ref[...] = jnp.zeros_like(sf_ref)
        ri = lax.broadcasted_iota(jnp.int32, (tr, tr), 0)
        ci = lax.broadcasted_iota(jnp.int32, (tr, tr), 1)
        rel = (ri - ci).astype(F32)
        for hh in range(2):
            lg_f = lg_ref[0, 2 * pair + hh]
            lg_b = lg_ref[1, 2 * pair + hh]
            d_ref[hh] = (jnp.exp(lg_f * jnp.maximum(rel, 0.0) + lg_b * jnp.maximum(-rel, 0.0))
                         + jnp.where(rel == 0.0, 1.0, 0.0))

    @pl.when(t < n_t)
    def _():
        blk = n_t - 1 - t
        for hh in range(2):
            lg_b = lg_ref[1, 2 * pair + hh]
            km = jnp.where(lane // RET_DK == hh, k, 0.0)
            vb = v_ref[:, hh * LANES:(hh + 1) * LANES].astype(BF16)
            cur = sbc_ref[hh]
            sb_ref[hh, blk] = cur
            zeta = jnp.exp(lg_b * row)
            sbc_ref[hh] = cur * jnp.exp(lg_b * tr) + _dot_tn((km * zeta).astype(BF16), vb)

    @pl.when(t >= n_t)
    def _():
        blk = t - n_t
        q = _rope_pair(q_ref[...].astype(F32), cos_t, sin_t, lane)
        qb = q.astype(BF16)
        for hh in range(2):
            lg_f = lg_ref[0, 2 * pair + hh]
            lg_b = lg_ref[1, 2 * pair + hh]
            km = jnp.where(lane // RET_DK == hh, k, 0.0)
            vb = v_ref[:, hh * LANES:(hh + 1) * LANES].astype(BF16)
            p = (_dot_nt(qb, km.astype(BF16)) * d_ref[hh]).astype(BF16)
            sf = sf_ref[hh]
            o = (_dot(p, vb)
                 + _dot((q * jnp.exp(lg_f * (row + 1.0))).astype(BF16), sf.astype(BF16))
                 + _dot((q * jnp.exp(lg_b * (tr - row))).astype(BF16), sb_ref[hh, blk].astype(BF16)))
            zeta = jnp.exp(lg_f * (tr - 1.0 - row))
            sf_ref[hh] = sf * jnp.exp(lg_f * tr) + _dot_tn((km * zeta).astype(BF16), vb)
            mu = jnp.mean(o, axis=-1, keepdims=True)
            oc = o - mu
            var = jnp.mean(oc * oc, axis=-1, keepdims=True)
            gate = g_ref[:, hh * LANES:(hh + 1) * LANES].astype(F32)
            y = oc * lax.rsqrt(var + EPS) * ng_ref[:, hh * LANES:(hh + 1) * LANES]
            o_ref[:, hh * LANES:(hh + 1) * LANES] = (y * (gate * jax.nn.sigmoid(gate))).astype(o_ref.dtype)


def retention_mixer(h3, norm_g, t_rows):
    bsz, s, _ = h3.shape
    n_t = s // t_rows
    cos_t, sin_t = _rope_tables(s)
    hidx = jnp.arange(RET_HEADS, dtype=F32)
    lg_f = jnp.log1p(-jnp.exp2(-5.0 - hidx))
    lg = jnp.stack([lg_f, lg_f[::-1]], axis=0)
    blk_of = lambda t: jnp.where(t < n_t, n_t - 1 - t, t - n_t)
    fwd_blk = lambda t: jnp.maximum(t - n_t, 0)
    return pl.pallas_call(
        functools.partial(_retention_kernel, t_rows=t_rows, n_t=n_t),
        out_shape=jax.ShapeDtypeStruct((bsz, s, RET_W), BF16),
        grid=(bsz, RET_HEADS // 2, 2 * n_t),
        in_specs=[pl.BlockSpec(memory_space=pltpu.SMEM),
                  pl.BlockSpec((None, t_rows, LANES), lambda b, p, t: (b, fwd_blk(t), CB_RET_Q + p)),
                  pl.BlockSpec((None, t_rows, LANES), lambda b, p, t: (b, blk_of(t), CB_RET_K + p)),
                  pl.BlockSpec((None, t_rows, 2 * LANES), lambda b, p, t: (b, blk_of(t), CB_RET_V // 2 + p)),
                  pl.BlockSpec((None, t_rows, 2 * LANES), lambda b, p, t: (b, fwd_blk(t), CB_RET_G // 2 + p)),
                  pl.BlockSpec((t_rows, LANES), lambda b, p, t: (blk_of(t), 0)),
                  pl.BlockSpec((t_rows, LANES), lambda b, p, t: (blk_of(t), 0)),
                  pl.BlockSpec((1, 2 * LANES), lambda b, p, t: (0, p))],
        out_specs=pl.BlockSpec((None, t_rows, 2 * LANES), lambda b, p, t: (b, fwd_blk(t), p)),
        scratch_shapes=[pltpu.VMEM((2, n_t, LANES, LANES), F32),
                        pltpu.VMEM((2, LANES, LANES), F32),
                        pltpu.VMEM((2, LANES, LANES), F32),
                        pltpu.VMEM((2, t_rows, t_rows), F32)],
        compiler_params=_cparams(("parallel", "parallel", "arbitrary")),
        name="retention",
    )(lg, h3, h3, h3, h3, cos_t, sin_t, norm_g.reshape(1, RET_W))


ATT_QB = 128


def _t5_bucket(rel):
    nb = REL_BUCKETS // 2
    max_exact = nb // 2
    sign_off = jnp.where(rel > 0, nb, 0)
    n = jnp.abs(rel)
    nf = jnp.maximum(n, 1).astype(F32)
    large = max_exact + (jnp.log(nf / max_exact) / math.log(REL_MAX_DIST / max_exact)
                         * (nb - max_exact)).astype(jnp.int32)
    large = jnp.minimum(large, nb - 1)
    return sign_off + jnp.where(n < max_exact, n, large)


def _dilated_kernel(tbl_ref, *refs, t_tok, s_len):
    n_g = len(DIL_GROUPS)
    grp = [refs[7 * g:7 * g + 7] for g in range(n_g)]
    qg_ref, kg_ref, bkt_ref, y_ref = refs[7 * n_g:7 * n_g + 4]
    og = refs[7 * n_g + 4:7 * n_g + 4 + n_g]
    lg = refs[7 * n_g + 4 + n_g:7 * n_g + 4 + 2 * n_g]
    bias_ref = refs[7 * n_g + 4 + 2 * n_g]
    slot = pl.program_id(1)
    t = pl.program_id(2)
    kw = ATT_QB + 2 * DIL_HALF

    @pl.when(t == 0)
    def _():
        for g in range(n_g):
            bkt = bkt_ref[g]
            bias = jnp.zeros((ATT_QB, kw), F32)
            for bucket in range(REL_BUCKETS):
                bias = jnp.where(bkt == bucket, tbl_ref[bucket, g * DIL_SLOTS + slot], bias)
            bias_ref[g] = bias

    ii = lax.broadcasted_iota(jnp.int32, (ATT_QB, kw), 0)
    jj = lax.broadcasted_iota(jnp.int32, (ATT_QB, kw), 1)
    in_band = jnp.abs(jj - DIL_HALF - ii) <= DIL_HALF
    qg = qg_ref[...] * (LANES ** -0.5)
    kg = kg_ref[...]

    for g, (_, dil) in enumerate(DIL_GROUPS):
        q_ref, kp_ref, k_ref, kn_ref, vp_ref, v_ref, vn_ref = grp[g]
        nq = t_tok // dil
        l_total = s_len // dil
        bias = bias_ref[g]

        def residue(r, dil=dil, nq=nq, l_total=l_total, bias=bias, g=g, q_ref=q_ref, kp_ref=kp_ref,
                    k_ref=k_ref, kn_ref=kn_ref, vp_ref=vp_ref, v_ref=v_ref, vn_ref=vn_ref):
            rows = lambda ref, n: ref[pl.ds(r, n, stride=dil), :]
            q = _rms_rows(rows(q_ref, nq), qg).astype(BF16)
            kall = jnp.concatenate([rows(kp_ref, DIL_HALF), rows(k_ref, nq), rows(kn_ref, DIL_HALF)], axis=0)
            kall = _rms_rows(kall, kg).astype(BF16)
            vall = jnp.concatenate([rows(vp_ref, DIL_HALF), rows(v_ref, nq), rows(vn_ref, DIL_HALF)],
                                   axis=0).astype(BF16)
            for qb in range(nq // ATT_QB):
                r0 = qb * ATT_QB
                key_idx = t * nq + r0 - DIL_HALF + jj
                valid = in_band & (key_idx >= 0) & (key_idx < l_total)
                s = _dot_nt(q[r0:r0 + ATT_QB], kall[r0:r0 + kw]) + bias
                s = jnp.where(valid, s, NEG_BIG)
                m = jnp.max(s, axis=-1, keepdims=True)
                p = jnp.exp(s - m)
                den = jnp.sum(p, axis=-1, keepdims=True)
                o = _dot(p.astype(BF16), vall[r0:r0 + kw]) / den
                dst = pl.ds(r + r0 * dil, ATT_QB, stride=dil)
                og[g][dst, :] = o
                lg[g][dst, :] = jnp.broadcast_to(m + jnp.log(den), (ATT_QB, LANES))

        if dil <= 4:
            for r in range(dil):
                residue(r)
        else:
            def body(r, carry, residue=residue):
                residue(r)
                return carry
            lax.fori_loop(0, dil, body, 0, unroll=4)

    l0, l1, l2 = lg[0][...], lg[1][...], lg[2][...]
    m = jnp.maximum(jnp.maximum(l0, l1), l2)
    w0, w1, w2 = jnp.exp(l0 - m), jnp.exp(l1 - m), jnp.exp(l2 - m)
    y = (w0 * og[0][...] + w1 * og[1][...] + w2 * og[2][...]) / (w0 + w1 + w2)
    y_ref[...] = y.astype(y_ref.dtype)


def dilated_mixer(hq_all, bsz, rel_bias, q_gain, k_gain, t_tok):
    n_g, m, _ = hq_all.shape
    s = m // bsz
    nb = s // t_tok
    cq, ck, cv = 0, DIL_SLOTS, 2 * DIL_SLOTS
    in_specs = [pl.BlockSpec(memory_space=pltpu.SMEM)]
    for g, (_, dil) in enumerate(DIL_GROUPS):
        halo = DIL_HALF * dil
        hpb = t_tok // halo
        n_halo = m // halo

        def main(cb, g=g):
            return pl.BlockSpec((None, t_tok, LANES), lambda b, sl, t: (g, b * nb + t, cb + sl))

        def prev(cb, g=g, halo=halo, hpb=hpb):
            return pl.BlockSpec((None, halo, LANES),
                                lambda b, sl, t: (g, jnp.maximum((b * nb + t) * hpb - 1, 0), cb + sl))

        def nxt(cb, g=g, halo=halo, hpb=hpb, n_halo=n_halo):
            return pl.BlockSpec((None, halo, LANES),
                                lambda b, sl, t: (g, jnp.minimum((b * nb + t + 1) * hpb, n_halo - 1), cb + sl))

        in_specs += [main(cq), prev(ck), main(ck), nxt(ck), prev(cv), main(cv), nxt(cv)]
    ii = jnp.arange(ATT_QB)[:, None]
    jj = jnp.arange(ATT_QB + 2 * DIL_HALF)[None, :]
    bkt = jnp.stack([_t5_bucket((jj - DIL_HALF - ii) * dil) for _, dil in DIL_GROUPS]).astype(jnp.int32)
    gain = pl.BlockSpec((1, LANES), lambda b, sl, t: (0, 0))
    in_specs += [gain, gain, pl.BlockSpec(bkt.shape, lambda b, sl, t: (0, 0, 0))]
    return pl.pallas_call(
        functools.partial(_dilated_kernel, t_tok=t_tok, s_len=s),
        out_shape=jax.ShapeDtypeStruct((m, DIL_W), BF16),
        grid=(bsz, DIL_SLOTS, nb),
        in_specs=in_specs,
        out_specs=pl.BlockSpec((t_tok, LANES), lambda b, sl, t: (b * nb + t, sl)),
        scratch_shapes=([pltpu.VMEM((t_tok, LANES), F32)] * (2 * n_g)
                        + [pltpu.VMEM((n_g, ATT_QB, ATT_QB + 2 * DIL_HALF), F32)]),
        compiler_params=_cparams(("parallel", "parallel", "arbitrary")),
        name="dilated_attn",
    )(rel_bias, *([hq_all] * (7 * n_g)), q_gain.reshape(1, LANES), k_gain.reshape(1, LANES), bkt)


def _tiles(bsz, s):
    m = bsz * s
    return dict(tm=min(1024, m), tm_small=min(512, m), hg_t=min(512, s), ret_t=min(512, s),
                att_t=min(2048, s))


def kernel(x, w_in, w_out, w_up, w_down, norm_mix, norm_mlp, hg_lb_fwd, hg_lb_bwd,
           hg_norm, ret_norm, q_norm, k_norm, rel_bias):
    bsz, s, d = x.shape
    m = bsz * s
    tl = _tiles(bsz, s)
    lb_f = jnp.cumsum(jax.nn.softmax(hg_lb_fwd.astype(F32), axis=0), axis=0)
    lb_b = jnp.cumsum(jax.nn.softmax(hg_lb_bwd.astype(F32), axis=0), axis=0)
    lb_f = lb_f - lb_f[0]
    lb_b = lb_b - lb_b[0]
    w_out_b = w_out.astype(BF16)
    w_up_b, w_down_b = w_up.astype(BF16), w_down.astype(BF16)

    xf = x.reshape(m, d)
    xn = rmsnorm(xf, norm_mix[0], tl["tm_small"])
    mix_cols = CB_DIL * LANES
    for l in range(DEPTH):
        h3 = in_proj(xn, w_in, l, tl["tm"], 1536, 0, mix_cols).reshape(bsz, s, mix_cols)
        hq_all = in_proj_groups(xn, w_in, l, tl["tm_small"], mix_cols)
        ya = hgrn2_mixer(h3, lb_f[l], lb_b[l], hg_norm[l], tl["hg_t"]).reshape(m, HG_W)
        yb = retention_mixer(h3, ret_norm[l], tl["ret_t"]).reshape(m, RET_W)
        yc = dilated_mixer(hq_all, bsz, rel_bias, q_norm[l], k_norm[l], tl["att_t"])
        xf, hn = out_proj(ya, yb, yc, w_out_b, l, xf, norm_mlp[l], tl["tm_small"])
        last = l == DEPTH - 1
        xf, xn = mlp(hn, w_up_b, w_down_b, l, xf, norm_mix[(l + 1) % DEPTH],
                     tl["tm_small"], 1024, not last)
    return xf.reshape(bsz, s, d)
```

```python
import functools
import math

import numpy as np
import jax
import jax.numpy as jnp
from jax import lax
from jax.experimental import pallas as pl
from jax.experimental.pallas import tpu as pltpu

F32 = jnp.float32
BF16 = jnp.bfloat16
EPS = 1e-6

D_MODEL = 2048
DEPTH = 4
LANES = 128

HG_HEADS = 6
HG_W = HG_HEADS * LANES
RET_HEADS = 6
RET_DK = 64
RET_W = RET_HEADS * LANES
ROPE_BASE = 10000.0
DIL_SLOTS = 4
DIL_GROUPS = ((128, 1), (512, 4), (2048, 16))
DIL_W = DIL_SLOTS * LANES
DIL_HALF = 64
MIX_W = HG_W + RET_W + DIL_W
D_FF = 4 * D_MODEL
REL_BUCKETS = 32
REL_MAX_DIST = 1024

CB_HG_Q, CB_HG_V, CB_HG_ZF, CB_HG_ZB, CB_HG_G = 0, 6, 12, 18, 24
CB_RET_Q, CB_RET_K = 30, 33
CB_RET_V, CB_RET_G = 36, 42
CB_DIL = 48
IN_W = 10752
IN_CB = IN_W // LANES

H_DTYPE = BF16
VMEM_LIMIT = 56 * 1024 * 1024

HG_CHUNK = 128
HG_PAR = 3
LOG2E = 1.4426950408889634
NEG_BIG = -1e30


def _cparams(sem):
    return pltpu.CompilerParams(dimension_semantics=sem, vmem_limit_bytes=VMEM_LIMIT)


def _dot(a, b):
    return jnp.dot(a, b, preferred_element_type=F32)


def _dot_nt(a, b):
    return lax.dot_general(a, b, (((1,), (1,)), ((), ())), preferred_element_type=F32)


def _dot_tn(a, b):
    return lax.dot_general(a, b, (((0,), (0,)), ((), ())), preferred_element_type=F32)


def _rms_rows(x, g):
    return x * lax.rsqrt(jnp.mean(x * x, axis=-1, keepdims=True) + EPS) * g


def _rmsnorm_kernel(x_ref, g_ref, o_ref):
    o_ref[...] = _rms_rows(x_ref[...], g_ref[...]).astype(o_ref.dtype)


def rmsnorm(x, g, tm):
    m, d = x.shape
    return pl.pallas_call(
        _rmsnorm_kernel,
        out_shape=jax.ShapeDtypeStruct((m, d), BF16),
        grid=(m // tm,),
        in_specs=[pl.BlockSpec((tm, d), lambda i: (i, 0)),
                  pl.BlockSpec((1, d), lambda i: (0, 0))],
        out_specs=pl.BlockSpec((tm, d), lambda i: (i, 0)),
        compiler_params=_cparams(("parallel",)),
        name="rmsnorm",
    )(x, g.reshape(1, d))


def _matmul_kernel(x_ref, w_ref, o_ref, wb_ref):
    @pl.when(pl.program_id(1) == 0)
    def _():
        wb_ref[...] = w_ref[...].astype(BF16)

    o_ref[...] = _dot(x_ref[...], wb_ref[...]).astype(o_ref.dtype)


def in_proj(xn, w_all, layer, tm, tn, col0, ncols):
    m, d = xn.shape
    j0 = col0 // tn
    return pl.pallas_call(
        _matmul_kernel,
        out_shape=jax.ShapeDtypeStruct((m, ncols), H_DTYPE),
        grid=(ncols // tn, m // tm),
        in_specs=[pl.BlockSpec((tm, d), lambda j, i: (i, 0)),
                  pl.BlockSpec((None, d, tn), lambda j, i: (layer, 0, j0 + j))],
        out_specs=pl.BlockSpec((tm, tn), lambda j, i: (i, j)),
        scratch_shapes=[pltpu.VMEM((d, tn), BF16)],
        compiler_params=_cparams(("parallel", "arbitrary")),
        name="in_proj",
    )(xn, w_all)


def in_proj_groups(xn, w_all, layer, tm, col0):
    m, d = xn.shape
    tn = 3 * DIL_W
    j0 = col0 // tn
    return pl.pallas_call(
        _matmul_kernel,
        out_shape=jax.ShapeDtypeStruct((len(DIL_GROUPS), m, tn), F32),
        grid=(len(DIL_GROUPS), m // tm),
        in_specs=[pl.BlockSpec((tm, d), lambda j, i: (i, 0)),
                  pl.BlockSpec((None, d, tn), lambda j, i: (layer, 0, j0 + j))],
        out_specs=pl.BlockSpec((None, tm, tn), lambda j, i: (j, i, 0)),
        scratch_shapes=[pltpu.VMEM((d, tn), BF16)],
        compiler_params=_cparams(("parallel", "arbitrary")),
        name="in_proj_groups",
    )(xn, w_all)


def _out_proj_kernel(ya_ref, yb_ref, yc_ref, w_ref, x_ref, g_ref, xo_ref, hn_ref):
    y = (_dot(ya_ref[...], w_ref[0:HG_W, :])
         + _dot(yb_ref[...], w_ref[HG_W:HG_W + RET_W, :])
         + _dot(yc_ref[...], w_ref[HG_W + RET_W:MIX_W, :]))
    xo = x_ref[...] + y
    xo_ref[...] = xo
    hn_ref[...] = _rms_rows(xo, g_ref[...]).astype(hn_ref.dtype)


def out_proj(ya, yb, yc, w_all, layer, x, g, tm):
    m, d = x.shape
    row = lambda i: (i, 0)
    return pl.pallas_call(
        _out_proj_kernel,
        out_shape=(jax.ShapeDtypeStruct((m, d), F32), jax.ShapeDtypeStruct((m, d), BF16)),
        grid=(m // tm,),
        in_specs=[pl.BlockSpec((tm, HG_W), row), pl.BlockSpec((tm, RET_W), row),
                  pl.BlockSpec((tm, DIL_W), row),
                  pl.BlockSpec((None, MIX_W, d), lambda i: (layer, 0, 0)),
                  pl.BlockSpec((tm, d), row),
                  pl.BlockSpec((1, d), lambda i: (0, 0))],
        out_specs=(pl.BlockSpec((tm, d), row), pl.BlockSpec((tm, d), row)),
        compiler_params=_cparams(("parallel",)),
        name="out_proj",
    )(ya, yb, yc, w_all, x, g.reshape(1, d))


def _mlp_kernel(h_ref, wu_ref, wd_ref, x_ref, g_ref, xo_ref, *rest, nj, emit_norm):
    j = pl.program_id(1)

    @pl.when(j == 0)
    def _():
        xo_ref[...] = x_ref[...]

    a = jnp.maximum(_dot(h_ref[...], wu_ref[...]), 0.0)
    a = (a * a).astype(BF16)
    xo_ref[...] += _dot(a, wd_ref[...])

    if emit_norm:
        xn_ref = rest[0]

        @pl.when(j == nj - 1)
        def _():
            xn_ref[...] = _rms_rows(xo_ref[...], g_ref[...]).astype(xn_ref.dtype)


def mlp(hn, wu_all, wd_all, layer, x, g, tm, tf, emit_norm):
    m, d = x.shape
    f = wu_all.shape[2]
    nj = f // tf
    row = lambda i, j: (i, 0)
    out_shape = [jax.ShapeDtypeStruct((m, d), F32)]
    out_specs = [pl.BlockSpec((tm, d), row)]
    if emit_norm:
        out_shape.append(jax.ShapeDtypeStruct((m, d), BF16))
        out_specs.append(pl.BlockSpec((tm, d), row))
    res = pl.pallas_call(
        functools.partial(_mlp_kernel, nj=nj, emit_norm=emit_norm),
        out_shape=tuple(out_shape),
        grid=(m // tm, nj),
        in_specs=[pl.BlockSpec((tm, d), row),
                  pl.BlockSpec((None, d, tf), lambda i, j: (layer, 0, j)),
                  pl.BlockSpec((None, tf, d), lambda i, j: (layer, j, 0)),
                  pl.BlockSpec((tm, d), row),
                  pl.BlockSpec((1, d), lambda i, j: (0, 0))],
        out_specs=tuple(out_specs),
        compiler_params=_cparams(("parallel", "arbitrary")),
        name="mlp",
    )(hn, wu_all, wd_all, x, g.reshape(1, d))
    return res if emit_norm else (res[0], None)


def _hg_constants(c):
    i = np.arange(c)[:, None]
    j = np.arange(c)[None, :]
    x = i ^ j
    lvl = np.where(x == 0, 0, np.floor(np.log2(np.maximum(x, 1))).astype(np.int64) + 1)
    tri_f = (j <= i).astype(np.float32)
    tri_r = (j >= i).astype(np.float32)
    lv_f = np.where(i >= j, lvl, -1).astype(np.int32)
    lv_r = np.where(i <= j, lvl, -1).astype(np.int32)
    return (jnp.asarray(tri_f, BF16), jnp.asarray(tri_r, BF16),
            jnp.asarray(lv_f), jnp.asarray(lv_r))


def _hg_ref_rows(b, m, reverse, rowid):
    c = b.shape[0]
    g = 2 * m
    r = m if reverse else m - 1
    if g >= 8:
        b3 = b.reshape(c // g, g, LANES)
        return jnp.broadcast_to(b3[:, r:r + 1, :], b3.shape).reshape(c, LANES)
    pos = rowid % g
    out = b
    for p in range(g):
        off = r - p
        if off == 0:
            continue
        out = jnp.where(pos == p, pltpu.roll(b, (-off) % c, 0), out)
    return out


def _neg_abs(x):
    bits = lax.bitcast_convert_type(x, jnp.uint32) | jnp.uint32(0x80000000)
    return lax.bitcast_convert_type(bits, F32)


def _hg_chunk(q, k, v, lf2, tri, lv, rowid, st, reverse):
    c = q.shape[0]
    hi = lf2.astype(BF16)
    lo = (lf2 - hi.astype(F32)).astype(BF16)
    b = _dot(tri, hi) + _dot(tri, lo)
    vb = v.astype(BF16)
    qb, kb = q.astype(BF16), k.astype(BF16)
    a = jnp.where(lv == 0, _dot_nt(qb, kb), 0.0)
    for ell in range(1, int(math.log2(c)) + 1):
        m = 2 ** (ell - 1)
        fac = jnp.exp2(_neg_abs(b - _hg_ref_rows(b, m, reverse, rowid)))
        a = jnp.where(lv == ell, _dot_nt((q * fac).astype(BF16), (k * fac).astype(BF16)), a)
    out = _dot(a.astype(BF16), vb) + _dot_nt((q * jnp.exp2(b)).astype(BF16), st.astype(BF16))
    b_edge = b[0:1, :] if reverse else b[c - 1:c, :]
    kk = (k * jnp.exp2(b_edge - b)).astype(BF16)
    st_new = st * jnp.exp2(b_edge) + _dot_tn(vb, kk)
    return out, st_new


def _hgrn2_kernel(q_ref, v_ref, zf_ref, zb_ref, g_ref, lbf_ref, lbb_ref, ng_ref,
                  trif_ref, trir_ref, lvf_ref, lvr_ref, o_ref, st_ref, ob_ref, *, t_rows, n_t):
    t = pl.program_id(2)
    c = HG_CHUNK
    n_c = t_rows // c
    rowid = lax.broadcasted_iota(jnp.int32, (c, LANES), 0)

    def sweep(z_ref, lb_ref, tri_ref, lv_ref, reverse, blk):
        tri = tri_ref[...]
        lv = lv_ref[...]

        def body(ci, carry):
            cc = (n_c - 1 - ci) if reverse else ci
            rows = pl.ds(pl.multiple_of(cc * c, c), c)
            srows = pl.ds(pl.multiple_of(blk * t_rows + cc * c, c), c)
            for hh in range(HG_PAR):
                cols = slice(hh * LANES, (hh + 1) * LANES)
                lb = lb_ref[:, cols]
                q = q_ref[rows, cols].astype(F32)
                v = v_ref[rows, cols].astype(F32)
                f = lb + (1.0 - lb) * jax.nn.sigmoid(z_ref[rows, cols].astype(F32))
                out, st_new = _hg_chunk(q, 1.0 - f, v, jnp.log(f) * LOG2E, tri, lv, rowid,
                                        st_ref[hh], reverse)
                st_ref[hh] = st_new
                if reverse:
                    ob_ref[srows, cols] = out
                else:
                    o = out + ob_ref[srows, cols]
                    gate = g_ref[rows, cols].astype(F32)
                    y = _rms_rows(o, ng_ref[:, cols]) * (gate * jax.nn.sigmoid(gate))
                    o_ref[rows, cols] = y.astype(o_ref.dtype)
            return carry

        lax.fori_loop(0, n_c, body, 0, unroll=True)

    @pl.when((t == 0) | (t == n_t))
    def _():
        st_ref[...] = jnp.zeros_like(st_ref)

    @pl.when(t < n_t)
    def _():
        sweep(zb_ref, lbb_ref, trir_ref, lvr_ref, True, n_t - 1 - t)

    @pl.when(t >= n_t)
    def _():
        sweep(zf_ref, lbf_ref, trif_ref, lvf_ref, False, t - n_t)


def hgrn2_mixer(h3, lb_f, lb_b, norm_g, t_rows):
    bsz, s, _ = h3.shape
    n_t = s // t_rows
    w = HG_PAR * LANES
    tri_f, tri_r, lv_f, lv_r = _hg_constants(HG_CHUNK)

    def both(cb):
        return pl.BlockSpec((None, t_rows, w),
                            lambda b, h, t: (b, jnp.where(t < n_t, n_t - 1 - t, t - n_t), cb // HG_PAR + h))

    def bwd_only(cb):
        return pl.BlockSpec((None, t_rows, w),
                            lambda b, h, t: (b, jnp.maximum(n_t - 1 - t, 0), cb // HG_PAR + h))

    def fwd_only(cb):
        return pl.BlockSpec((None, t_rows, w),
                            lambda b, h, t: (b, jnp.maximum(t - n_t, 0), cb // HG_PAR + h))

    head_row = pl.BlockSpec((1, w), lambda b, h, t: (0, h))
    const = pl.BlockSpec((HG_CHUNK, HG_CHUNK), lambda b, h, t: (0, 0))
    return pl.pallas_call(
        functools.partial(_hgrn2_kernel, t_rows=t_rows, n_t=n_t),
        out_shape=jax.ShapeDtypeStruct((bsz, s, HG_W), BF16),
        grid=(bsz, HG_HEADS // HG_PAR, 2 * n_t),
        in_specs=[both(CB_HG_Q), both(CB_HG_V), fwd_only(CB_HG_ZF), bwd_only(CB_HG_ZB),
                  fwd_only(CB_HG_G), head_row, head_row, head_row, const, const, const, const],
        out_specs=pl.BlockSpec((None, t_rows, w), lambda b, h, t: (b, jnp.maximum(t - n_t, 0), h)),
        scratch_shapes=[pltpu.VMEM((HG_PAR, LANES, LANES), F32), pltpu.VMEM((s, w), F32)],
        compiler_params=_cparams(("parallel", "parallel", "arbitrary")),
        name="hgrn2",
    )(h3, h3, h3, h3, h3, lb_f.reshape(1, HG_W), lb_b.reshape(1, HG_W),
      norm_g.reshape(1, HG_W), tri_f, tri_r, lv_f, lv_r)


def _rope_tables(s):
    half = RET_DK // 2
    inv = ROPE_BASE ** (-jnp.arange(half, dtype=F32) / half)
    ang = jnp.arange(s, dtype=F32)[:, None] * inv[None, :]
    cos, sin = jnp.cos(ang), jnp.sin(ang)
    cos_t = jnp.concatenate([cos, cos, cos, cos], axis=-1)
    sin_t = jnp.concatenate([-sin, sin, -sin, sin], axis=-1)
    return cos_t, sin_t


def _rope_pair(x, cos_t, sin_t, lane):
    half = RET_DK // 2
    swapped = jnp.where(lane % RET_DK < half, pltpu.roll(x, LANES - half, 1), pltpu.roll(x, half, 1))
    return x * cos_t + swapped * sin_t


def _retention_kernel(lg_ref, q_ref, k_ref, v_ref, g_ref, cos_ref, sin_ref, ng_ref, o_ref,
                      sb_ref, sbc_ref, sf_ref, d_ref, *, t_rows, n_t):
    pair = pl.program_id(1)
    t = pl.program_id(2)
    tr = t_rows
    lane = lax.broadcasted_iota(jnp.int32, (tr, LANES), 1)
    row = lax.broadcasted_iota(jnp.int32, (tr, LANES), 0).astype(F32)
    cos_t = cos_ref[...]
    sin_t = sin_ref[...]
    k = _rope_pair(k_ref[...].astype(F32), cos_t, sin_t, lane) * (RET_DK ** -0.5)

    @pl.when(t == 0)
    def _():
        sbc_ref[...] = jnp.zeros_like(sbc_ref)

    @pl.when(t == n_t)
    def _():
        sf_ref[...] = jnp.zeros_like(sf_ref)
        ri = lax.broadcasted_iota(jnp.int32, (tr, tr), 0)
        ci = lax.broadcasted_iota(jnp.int32, (tr, tr), 1)
        rel = (ri - ci).astype(F32)
        for hh in range(2):
            lg_f = lg_ref[0, 2 * pair + hh]
            lg_b = lg_ref[1, 2 * pair + hh]
            d_ref[hh] = (jnp.exp(lg_f * jnp.maximum(rel, 0.0) + lg_b * jnp.maximum(-rel, 0.0))
                         + jnp.where(rel == 0.0, 1.0, 0.0))

    @pl.when(t < n_t)
    def _():
        blk = n_t - 1 - t
        for hh in range(2):
            lg_b = lg_ref[1, 2 * pair + hh]
            km = jnp.where(lane // RET_DK == hh, k, 0.0)
            vb = v_ref[:, hh * LANES:(hh + 1) * LANES].astype(BF16)
            cur = sbc_ref[hh]
            sb_ref[hh, blk] = cur
            zeta = jnp.exp(lg_b * row)
            sbc_ref[hh] = cur * jnp.exp(lg_b * tr) + _dot_tn((km * zeta).astype(BF16), vb)

    @pl.when(t >= n_t)
    def _():
        blk = t - n_t
        q = _rope_pair(q_ref[...].astype(F32), cos_t, sin_t, lane)
        qb = q.astype(BF16)
        for hh in range(2):
            lg_f = lg_ref[0, 2 * pair + hh]
            lg_b = lg_ref[1, 2 * pair + hh]
            km = jnp.where(lane // RET_DK == hh, k, 0.0)
            vb = v_ref[:, hh * LANES:(hh + 1) * LANES].astype(BF16)
            p = (_dot_nt(qb, km.astype(BF16)) * d_ref[hh]).astype(BF16)
            sf = sf_ref[hh]
            o = (_dot(p, vb)
                 + _dot((q * jnp.exp(lg_f * (row + 1.0))).astype(BF16), sf.astype(BF16))
                 + _dot((q * jnp.exp(lg_b * (tr - row))).astype(BF16), sb_ref[hh, blk].astype(BF16)))
            zeta = jnp.exp(lg_f * (tr - 1.0 - row))
            sf_ref[hh] = sf * jnp.exp(lg_f * tr) + _dot_tn((km * zeta).astype(BF16), vb)
            mu = jnp.mean(o, axis=-1, keepdims=True)
            oc = o - mu
            var = jnp.mean(oc * oc, axis=-1, keepdims=True)
            gate = g_ref[:, hh * LANES:(hh + 1) * LANES].astype(F32)
            y = oc * lax.rsqrt(var + EPS) * ng_ref[:, hh * LANES:(hh + 1) * LANES]
            o_ref[:, hh * LANES:(hh + 1) * LANES] = (y * (gate * jax.nn.sigmoid(gate))).astype(o_ref.dtype)


def retention_mixer(h3, norm_g, t_rows):
    bsz, s, _ = h3.shape
    n_t = s // t_rows
    cos_t, sin_t = _rope_tables(s)
    hidx = jnp.arange(RET_HEADS, dtype=F32)
    lg_f = jnp.log1p(-jnp.exp2(-5.0 - hidx))
    lg = jnp.stack([lg_f, lg_f[::-1]], axis=0)
    blk_of = lambda t: jnp.where(t < n_t, n_t - 1 - t, t - n_t)
    fwd_blk = lambda t: jnp.maximum(t - n_t, 0)
    return pl.pallas_call(
        functools.partial(_retention_kernel, t_rows=t_rows, n_t=n_t),
        out_shape=jax.ShapeDtypeStruct((bsz, s, RET_W), BF16),
        grid=(bsz, RET_HEADS // 2, 2 * n_t),
        in_specs=[pl.BlockSpec(memory_space=pltpu.SMEM),
                  pl.BlockSpec((None, t_rows, LANES), lambda b, p, t: (b, fwd_blk(t), CB_RET_Q + p)),
                  pl.BlockSpec((None, t_rows, LANES), lambda b, p, t: (b, blk_of(t), CB_RET_K + p)),
                  pl.BlockSpec((None, t_rows, 2 * LANES), lambda b, p, t: (b, blk_of(t), CB_RET_V // 2 + p)),
                  pl.BlockSpec((None, t_rows, 2 * LANES), lambda b, p, t: (b, fwd_blk(t), CB_RET_G // 2 + p)),
                  pl.BlockSpec((t_rows, LANES), lambda b, p, t: (blk_of(t), 0)),
                  pl.BlockSpec((t_rows, LANES), lambda b, p, t: (blk_of(t), 0)),
                  pl.BlockSpec((1, 2 * LANES), lambda b, p, t: (0, p))],
        out_specs=pl.BlockSpec((None, t_rows, 2 * LANES), lambda b, p, t: (b, fwd_blk(t), p)),
        scratch_shapes=[pltpu.VMEM((2, n_t, LANES, LANES), F32),
                        pltpu.VMEM((2, LANES, LANES), F32),
                        pltpu.VMEM((2, LANES, LANES), F32),
                        pltpu.VMEM((2, t_rows, t_rows), F32)],
        compiler_params=_cparams(("parallel", "parallel", "arbitrary")),
        name="retention",
    )(lg, h3, h3, h3, h3, cos_t, sin_t, norm_g.reshape(1, RET_W))


ATT_QB = 128


def _t5_bucket(rel):
    nb = REL_BUCKETS // 2
    max_exact = nb // 2
    sign_off = jnp.where(rel > 0, nb, 0)
    n = jnp.abs(rel)
    nf = jnp.maximum(n, 1).astype(F32)
    large = max_exact + (jnp.log(nf / max_exact) / math.log(REL_MAX_DIST / max_exact)
                         * (nb - max_exact)).astype(jnp.int32)
    large = jnp.minimum(large, nb - 1)
    return sign_off + jnp.where(n < max_exact, n, large)


def _dilated_kernel(tbl_ref, *refs, t_tok, s_len):
    n_g = len(DIL_GROUPS)
    grp = [refs[7 * g:7 * g + 7] for g in range(n_g)]
    qg_ref, kg_ref, bkt_ref, y_ref = refs[7 * n_g:7 * n_g + 4]
    og = refs[7 * n_g + 4:7 * n_g + 4 + n_g]
    lg = refs[7 * n_g + 4 + n_g:7 * n_g + 4 + 2 * n_g]
    bias_ref = refs[7 * n_g + 4 + 2 * n_g]
    slot = pl.program_id(1)
    t = pl.program_id(2)
    kw = ATT_QB + 2 * DIL_HALF

    @pl.when(t == 0)
    def _():
        for g in range(n_g):
            bkt = bkt_ref[g]
            bias = jnp.zeros((ATT_QB, kw), F32)
            for bucket in range(REL_BUCKETS):
                bias = jnp.where(bkt == bucket, tbl_ref[bucket, g * DIL_SLOTS + slot], bias)
            bias_ref[g] = bias

    ii = lax.broadcasted_iota(jnp.int32, (ATT_QB, kw), 0)
    jj = lax.broadcasted_iota(jnp.int32, (ATT_QB, kw), 1)
    in_band = jnp.abs(jj - DIL_HALF - ii) <= DIL_HALF
    qg = qg_ref[...] * (LANES ** -0.5)
    kg = kg_ref[...]

    for g, (_, dil) in enumerate(DIL_GROUPS):
        q_ref, kp_ref, k_ref, kn_ref, vp_ref, v_ref, vn_ref = grp[g]
        nq = t_tok // dil
        l_total = s_len // dil
        bias = bias_ref[g]

        def residue(r, dil=dil, nq=nq, l_total=l_total, bias=bias, g=g, q_ref=q_ref, kp_ref=kp_ref,
                    k_ref=k_ref, kn_ref=kn_ref, vp_ref=vp_ref, v_ref=v_ref, vn_ref=vn_ref):
            rows = lambda ref, n: ref[pl.ds(r, n, stride=dil), :]
            q = _rms_rows(rows(q_ref, nq), qg).astype(BF16)
            kall = jnp.concatenate([rows(kp_ref, DIL_HALF), rows(k_ref, nq), rows(kn_ref, DIL_HALF)], axis=0)
            kall = _rms_rows(kall, kg).astype(BF16)
            vall = jnp.concatenate([rows(vp_ref, DIL_HALF), rows(v_ref, nq), rows(vn_ref, DIL_HALF)],
                                   axis=0).astype(BF16)
            for qb in range(nq // ATT_QB):
                r0 = qb * ATT_QB
                key_idx = t * nq + r0 - DIL_HALF + jj
                valid = in_band & (key_idx >= 0) & (key_idx < l_total)
                s = _dot_nt(q[r0:r0 + ATT_QB], kall[r0:r0 + kw]) + bias
                s = jnp.where(valid, s, NEG_BIG)
                m = jnp.max(s, axis=-1, keepdims=True)
                p = jnp.exp(s - m)
                den = jnp.sum(p, axis=-1, keepdims=True)
                o = _dot(p.astype(BF16), vall[r0:r0 + kw]) / den
                dst = pl.ds(r + r0 * dil, ATT_QB, stride=dil)
                og[g][dst, :] = o
                lg[g][dst, :] = jnp.broadcast_to(m + jnp.log(den), (ATT_QB, LANES))

        if dil <= 4:
            for r in range(dil):
                residue(r)
        else:
            def body(r, carry, residue=residue):
                residue(r)
                return carry
            lax.fori_loop(0, dil, body, 0, unroll=4)

    l0, l1, l2 = lg[0][...], lg[1][...], lg[2][...]
    m = jnp.maximum(jnp.maximum(l0, l1), l2)
    w0, w1, w2 = jnp.exp(l0 - m), jnp.exp(l1 - m), jnp.exp(l2 - m)
    y = (w0 * og[0][...] + w1 * og[1][...] + w2 * og[2][...]) / (w0 + w1 + w2)
    y_ref[...] = y.astype(y_ref.dtype)


def dilated_mixer(hq_all, bsz, rel_bias, q_gain, k_gain, t_tok):
    n_g, m, _ = hq_all.shape
    s = m // bsz
    nb = s // t_tok
    cq, ck, cv = 0, DIL_SLOTS, 2 * DIL_SLOTS
    in_specs = [pl.BlockSpec(memory_space=pltpu.SMEM)]
    for g, (_, dil) in enumerate(DIL_GROUPS):
        halo = DIL_HALF * dil
        hpb = t_tok // halo
        n_halo = m // halo

        def main(cb, g=g):
            return pl.BlockSpec((None, t_tok, LANES), lambda b, sl, t: (g, b * nb + t, cb + sl))

        def prev(cb, g=g, halo=halo, hpb=hpb):
            return pl.BlockSpec((None, halo, LANES),
                                lambda b, sl, t: (g, jnp.maximum((b * nb + t) * hpb - 1, 0), cb + sl))

        def nxt(cb, g=g, halo=halo, hpb=hpb, n_halo=n_halo):
            return pl.BlockSpec((None, halo, LANES),
                                lambda b, sl, t: (g, jnp.minimum((b * nb + t + 1) * hpb, n_halo - 1), cb + sl))

        in_specs += [main(cq), prev(ck), main(ck), nxt(ck), prev(cv), main(cv), nxt(cv)]
    ii = jnp.arange(ATT_QB)[:, None]
    jj = jnp.arange(ATT_QB + 2 * DIL_HALF)[None, :]
    bkt = jnp.stack([_t5_bucket((jj - DIL_HALF - ii) * dil) for _, dil in DIL_GROUPS]).astype(jnp.int32)
    gain = pl.BlockSpec((1, LANES), lambda b, sl, t: (0, 0))
    in_specs += [gain, gain, pl.BlockSpec(bkt.shape, lambda b, sl, t: (0, 0, 0))]
    return pl.pallas_call(
        functools.partial(_dilated_kernel, t_tok=t_tok, s_len=s),
        out_shape=jax.ShapeDtypeStruct((m, DIL_W), BF16),
        grid=(bsz, DIL_SLOTS, nb),
        in_specs=in_specs,
        out_specs=pl.BlockSpec((t_tok, LANES), lambda b, sl, t: (b * nb + t, sl)),
        scratch_shapes=([pltpu.VMEM((t_tok, LANES), F32)] * (2 * n_g)
                        + [pltpu.VMEM((n_g, ATT_QB, ATT_QB + 2 * DIL_HALF), F32)]),
        compiler_params=_cparams(("parallel", "parallel", "arbitrary")),
        name="dilated_attn",
    )(rel_bias, *([hq_all] * (7 * n_g)), q_gain.reshape(1, LANES), k_gain.reshape(1, LANES), bkt)


def _tiles(bsz, s):
    m = bsz * s
    return dict(tm=min(1024, m), tm_small=min(512, m), hg_t=min(512, s), ret_t=min(512, s),
                att_t=min(2048, s))


def kernel(x, w_in, w_out, w_up, w_down, norm_mix, norm_mlp, hg_lb_fwd, hg_lb_bwd,
           hg_norm, ret_norm, q_norm, k_norm, rel_bias):
    bsz, s, d = x.shape
    m = bsz * s
    tl = _tiles(bsz, s)
    lb_f = jnp.cumsum(jax.nn.softmax(hg_lb_fwd.astype(F32), axis=0), axis=0)
    lb_b = jnp.cumsum(jax.nn.softmax(hg_lb_bwd.astype(F32), axis=0), axis=0)
    lb_f = lb_f - lb_f[0]
    lb_b = lb_b - lb_b[0]
    w_out_b = w_out.astype(BF16)
    w_up_b, w_down_b = w_up.astype(BF16), w_down.astype(BF16)

    xf = x.reshape(m, d)
    xn = rmsnorm(xf, norm_mix[0], tl["tm_small"])
    mix_cols = CB_DIL * LANES
    for l in range(DEPTH):
        h3 = in_proj(xn, w_in, l, tl["tm"], 1536, 0, mix_cols).reshape(bsz, s, mix_cols)
        hq_all = in_proj_groups(xn, w_in, l, tl["tm"], mix_cols)
        ya = hgrn2_mixer(h3, lb_f[l], lb_b[l], hg_norm[l], tl["hg_t"]).reshape(m, HG_W)
        yb = retention_mixer(h3, ret_norm[l], tl["ret_t"]).reshape(m, RET_W)
        yc = dilated_mixer(hq_all, bsz, rel_bias, q_norm[l], k_norm[l], tl["att_t"])
        xf, hn = out_proj(ya, yb, yc, w_out_b, l, xf, norm_mlp[l], tl["tm_small"])
        last = l == DEPTH - 1
        xf, xn = mlp(hn, w_up_b, w_down_b, l, xf, norm_mix[(l + 1) % DEPTH],
                     tl["tm_small"], 1024, not last)
    return xf.reshape(bsz, s, d)
```

```python
import functools
import math

import numpy as np
import jax
import jax.numpy as jnp
from jax import lax
from jax.experimental import pallas as pl
from jax.experimental.pallas import tpu as pltpu

F32 = jnp.float32
BF16 = jnp.bfloat16
EPS = 1e-6

D_MODEL = 2048
DEPTH = 4
LANES = 128

HG_HEADS = 6
HG_W = HG_HEADS * LANES
RET_HEADS = 6
RET_DK = 64
RET_W = RET_HEADS * LANES
ROPE_BASE = 10000.0
DIL_SLOTS = 4
DIL_GROUPS = ((128, 1), (512, 4), (2048, 16))
DIL_W = DIL_SLOTS * LANES
DIL_HALF = 64
MIX_W = HG_W + RET_W + DIL_W
D_FF = 4 * D_MODEL
REL_BUCKETS = 32
REL_MAX_DIST = 1024

CB_HG_Q, CB_HG_V, CB_HG_ZF, CB_HG_ZB, CB_HG_G = 0, 6, 12, 18, 24
CB_RET_Q, CB_RET_K = 30, 33
CB_RET_V, CB_RET_G = 36, 42
CB_DIL = 48
IN_W = 10752
IN_CB = IN_W // LANES

H_DTYPE = BF16
VMEM_LIMIT = 56 * 1024 * 1024

HG_CHUNK = 128
HG_PAR = 3
LOG2E = 1.4426950408889634
NEG_BIG = -1e30


def _cparams(sem):
    return pltpu.CompilerParams(dimension_semantics=sem, vmem_limit_bytes=VMEM_LIMIT)


def _dot(a, b):
    return jnp.dot(a, b, preferred_element_type=F32)


def _dot_nt(a, b):
    return lax.dot_general(a, b, (((1,), (1,)), ((), ())), preferred_element_type=F32)


def _dot_tn(a, b):
    return lax.dot_general(a, b, (((0,), (0,)), ((), ())), preferred_element_type=F32)


def _rms_rows(x, g):
    return x * lax.rsqrt(jnp.mean(x * x, axis=-1, keepdims=True) + EPS) * g


def _rmsnorm_kernel(x_ref, g_ref, o_ref):
    o_ref[...] = _rms_rows(x_ref[...], g_ref[...]).astype(o_ref.dtype)


def rmsnorm(x, g, tm):
    m, d = x.shape
    return pl.pallas_call(
        _rmsnorm_kernel,
        out_shape=jax.ShapeDtypeStruct((m, d), BF16),
        grid=(m // tm,),
        in_specs=[pl.BlockSpec((tm, d), lambda i: (i, 0)),
                  pl.BlockSpec((1, d), lambda i: (0, 0))],
        out_specs=pl.BlockSpec((tm, d), lambda i: (i, 0)),
        compiler_params=_cparams(("parallel",)),
        name="rmsnorm",
    )(x, g.reshape(1, d))


def _matmul_kernel(x_ref, w_ref, o_ref, wb_ref):
    @pl.when(pl.program_id(1) == 0)
    def _():
        wb_ref[...] = w_ref[...].astype(BF16)

    o_ref[...] = _dot(x_ref[...], wb_ref[...]).astype(o_ref.dtype)


def in_proj(xn, w_all, layer, tm, tn, col0, ncols):
    m, d = xn.shape
    j0 = col0 // tn
    return pl.pallas_call(
        _matmul_kernel,
        out_shape=jax.ShapeDtypeStruct((m, ncols), H_DTYPE),
        grid=(ncols // tn, m // tm),
        in_specs=[pl.BlockSpec((tm, d), lambda j, i: (i, 0)),
                  pl.BlockSpec((None, d, tn), lambda j, i: (layer, 0, j0 + j))],
        out_specs=pl.BlockSpec((tm, tn), lambda j, i: (i, j)),
        scratch_shapes=[pltpu.VMEM((d, tn), BF16)],
        compiler_params=_cparams(("parallel", "arbitrary")),
        name="in_proj",
    )(xn, w_all)


def in_proj_groups(xn, w_all, layer, tm, col0):
    m, d = xn.shape
    tn = 3 * DIL_W
    j0 = col0 // tn
    return pl.pallas_call(
        _matmul_kernel,
        out_shape=jax.ShapeDtypeStruct((len(DIL_GROUPS), m, tn), F32),
        grid=(len(DIL_GROUPS), m // tm),
        in_specs=[pl.BlockSpec((tm, d), lambda j, i: (i, 0)),
                  pl.BlockSpec((None, d, tn), lambda j, i: (layer, 0, j0 + j))],
        out_specs=pl.BlockSpec((None, tm, tn), lambda j, i: (j, i, 0)),
        scratch_shapes=[pltpu.VMEM((d, tn), BF16)],
        compiler_params=_cparams(("parallel", "arbitrary")),
        name="in_proj_groups",
    )(xn, w_all)


def _out_proj_kernel(ya_ref, yb_ref, yc_ref, w_ref, x_ref, g_ref, xo_ref, hn_ref):
    y = (_dot(ya_ref[...], w_ref[0:HG_W, :])
         + _dot(yb_ref[...], w_ref[HG_W:HG_W + RET_W, :])
         + _dot(yc_ref[...], w_ref[HG_W + RET_W:MIX_W, :]))
    xo = x_ref[...] + y
    xo_ref[...] = xo
    hn_ref[...] = _rms_rows(xo, g_ref[...]).astype(hn_ref.dtype)


def out_proj(ya, yb, yc, w_all, layer, x, g, tm):
    m, d = x.shape
    row = lambda i: (i, 0)
    return pl.pallas_call(
        _out_proj_kernel,
        out_shape=(jax.ShapeDtypeStruct((m, d), F32), jax.ShapeDtypeStruct((m, d), BF16)),
        grid=(m // tm,),
        in_specs=[pl.BlockSpec((tm, HG_W), row), pl.BlockSpec((tm, RET_W), row),
                  pl.BlockSpec((tm, DIL_W), row),
                  pl.BlockSpec((None, MIX_W, d), lambda i: (layer, 0, 0)),
                  pl.BlockSpec((tm, d), row),
                  pl.BlockSpec((1, d), lambda i: (0, 0))],
        out_specs=(pl.BlockSpec((tm, d), row), pl.BlockSpec((tm, d), row)),
        compiler_params=_cparams(("parallel",)),
        name="out_proj",
    )(ya, yb, yc, w_all, x, g.reshape(1, d))


def _mlp_kernel(h_ref, wu_ref, wd_ref, x_ref, g_ref, xo_ref, *rest, nj, emit_norm):
    j = pl.program_id(1)

    @pl.when(j == 0)
    def _():
        xo_ref[...] = x_ref[...]

    a = jnp.maximum(_dot(h_ref[...], wu_ref[...]), 0.0)
    a = (a * a).astype(BF16)
    xo_ref[...] += _dot(a, wd_ref[...])

    if emit_norm:
        xn_ref = rest[0]

        @pl.when(j == nj - 1)
        def _():
            xn_ref[...] = _rms_rows(xo_ref[...], g_ref[...]).astype(xn_ref.dtype)


def mlp(hn, wu_all, wd_all, layer, x, g, tm, tf, emit_norm):
    m, d = x.shape
    f = wu_all.shape[2]
    nj = f // tf
    row = lambda i, j: (i, 0)
    out_shape = [jax.ShapeDtypeStruct((m, d), F32)]
    out_specs = [pl.BlockSpec((tm, d), row)]
    if emit_norm:
        out_shape.append(jax.ShapeDtypeStruct((m, d), BF16))
        out_specs.append(pl.BlockSpec((tm, d), row))
    res = pl.pallas_call(
        functools.partial(_mlp_kernel, nj=nj, emit_norm=emit_norm),
        out_shape=tuple(out_shape),
        grid=(m // tm, nj),
        in_specs=[pl.BlockSpec((tm, d), row),
                  pl.BlockSpec((None, d, tf), lambda i, j: (layer, 0, j)),
                  pl.BlockSpec((None, tf, d), lambda i, j: (layer, j, 0)),
                  pl.BlockSpec((tm, d), row),
                  pl.BlockSpec((1, d), lambda i, j: (0, 0))],
        out_specs=tuple(out_specs),
        compiler_params=_cparams(("parallel", "arbitrary")),
        name="mlp",
    )(hn, wu_all, wd_all, x, g.reshape(1, d))
    return res if emit_norm else (res[0], None)


def _hg_constants(c):
    i = np.arange(c)[:, None]
    j = np.arange(c)[None, :]
    x = i ^ j
    lvl = np.where(x == 0, 0, np.floor(np.log2(np.maximum(x, 1))).astype(np.int64) + 1)
    tri_f = (j <= i).astype(np.float32)
    tri_r = (j >= i).astype(np.float32)
    lv_f = np.where(i >= j, lvl, -1).astype(np.int32)
    lv_r = np.where(i <= j, lvl, -1).astype(np.int32)
    return (jnp.asarray(tri_f, BF16), jnp.asarray(tri_r, BF16),
            jnp.asarray(lv_f), jnp.asarray(lv_r))


def _hg_ref_rows(b, m, reverse, rowid):
    c = b.shape[0]
    g = 2 * m
    r = m if reverse else m - 1
    if g >= 8:
        b3 = b.reshape(c // g, g, LANES)
        return jnp.broadcast_to(b3[:, r:r + 1, :], b3.shape).reshape(c, LANES)
    pos = rowid % g
    out = b
    for p in range(g):
        off = r - p
        if off == 0:
            continue
        out = jnp.where(pos == p, pltpu.roll(b, (-off) % c, 0), out)
    return out


def _neg_abs(x):
    bits = lax.bitcast_convert_type(x, jnp.uint32) | jnp.uint32(0x80000000)
    return lax.bitcast_convert_type(bits, F32)


def _hg_chunk(q, k, v, lf2, tri, lv, rowid, st, reverse):
    c = q.shape[0]
    hi = lf2.astype(BF16)
    lo = (lf2 - hi.astype(F32)).astype(BF16)
    b = _dot(tri, hi) + _dot(tri, lo)
    vb = v.astype(BF16)
    qb, kb = q.astype(BF16), k.astype(BF16)
    a = jnp.where(lv == 0, _dot_nt(qb, kb), 0.0)
    for ell in range(1, int(math.log2(c)) + 1):
        m = 2 ** (ell - 1)
        fac = jnp.exp2(_neg_abs(b - _hg_ref_rows(b, m, reverse, rowid)))
        a = jnp.where(lv == ell, _dot_nt((q * fac).astype(BF16), (k * fac).astype(BF16)), a)
    out = _dot(a.astype(BF16), vb) + _dot_nt((q * jnp.exp2(b)).astype(BF16), st.astype(BF16))
    b_edge = b[0:1, :] if reverse else b[c - 1:c, :]
    kk = (k * jnp.exp2(b_edge - b)).astype(BF16)
    st_new = st * jnp.exp2(b_edge) + _dot_tn(vb, kk)
    return out, st_new


def _hgrn2_kernel(q_ref, v_ref, zf_ref, zb_ref, g_ref, lbf_ref, lbb_ref, ng_ref,
                  trif_ref, trir_ref, lvf_ref, lvr_ref, o_ref, st_ref, ob_ref, *, t_rows, n_t):
    t = pl.program_id(2)
    c = HG_CHUNK
    n_c = t_rows // c
    rowid = lax.broadcasted_iota(jnp.int32, (c, LANES), 0)

    def sweep(z_ref, lb_ref, tri_ref, lv_ref, reverse, blk):
        tri = tri_ref[...]
        lv = lv_ref[...]

        def body(ci, carry):
            cc = (n_c - 1 - ci) if reverse else ci
            rows = pl.ds(pl.multiple_of(cc * c, c), c)
            srows = pl.ds(pl.multiple_of(blk * t_rows + cc * c, c), c)
            for hh in range(HG_PAR):
                cols = slice(hh * LANES, (hh + 1) * LANES)
                lb = lb_ref[:, cols]
                q = q_ref[rows, cols].astype(F32)
                v = v_ref[rows, cols].astype(F32)
                f = lb + (1.0 - lb) * jax.nn.sigmoid(z_ref[rows, cols].astype(F32))
                out, st_new = _hg_chunk(q, 1.0 - f, v, jnp.log(f) * LOG2E, tri, lv, rowid,
                                        st_ref[hh], reverse)
                st_ref[hh] = st_new
                if reverse:
                    ob_ref[srows, cols] = out
                else:
                    o = out + ob_ref[srows, cols]
                    gate = g_ref[rows, cols].astype(F32)
                    y = _rms_rows(o, ng_ref[:, cols]) * (gate * jax.nn.sigmoid(gate))
                    o_ref[rows, cols] = y.astype(o_ref.dtype)
            return carry

        lax.fori_loop(0, n_c, body, 0, unroll=True)

    @pl.when((t == 0) | (t == n_t))
    def _():
        st_ref[...] = jnp.zeros_like(st_ref)

    @pl.when(t < n_t)
    def _():
        sweep(zb_ref, lbb_ref, trir_ref, lvr_ref, True, n_t - 1 - t)

    @pl.when(t >= n_t)
    def _():
        sweep(zf_ref, lbf_ref, trif_ref, lvf_ref, False, t - n_t)


def hgrn2_mixer(h3, lb_f, lb_b, norm_g, t_rows):
    bsz, s, _ = h3.shape
    n_t = s // t_rows
    w = HG_PAR * LANES
    tri_f, tri_r, lv_f, lv_r = _hg_constants(HG_CHUNK)

    def both(cb):
        return pl.BlockSpec((None, t_rows, w),
                            lambda b, h, t: (b, jnp.where(t < n_t, n_t - 1 - t, t - n_t), cb // HG_PAR + h))

    def bwd_only(cb):
        return pl.BlockSpec((None, t_rows, w),
                            lambda b, h, t: (b, jnp.maximum(n_t - 1 - t, 0), cb // HG_PAR + h))

    def fwd_only(cb):
        return pl.BlockSpec((None, t_rows, w),
                            lambda b, h, t: (b, jnp.maximum(t - n_t, 0), cb // HG_PAR + h))

    head_row = pl.BlockSpec((1, w), lambda b, h, t: (0, h))
    const = pl.BlockSpec((HG_CHUNK, HG_CHUNK), lambda b, h, t: (0, 0))
    return pl.pallas_call(
        functools.partial(_hgrn2_kernel, t_rows=t_rows, n_t=n_t),
        out_shape=jax.ShapeDtypeStruct((bsz, s, HG_W), BF16),
        grid=(bsz, HG_HEADS // HG_PAR, 2 * n_t),
        in_specs=[both(CB_HG_Q), both(CB_HG_V), fwd_only(CB_HG_ZF), bwd_only(CB_HG_ZB),
                  fwd_only(CB_HG_G), head_row, head_row, head_row, const, const, const, const],
        out_specs=pl.BlockSpec((None, t_rows, w), lambda b, h, t: (b, jnp.maximum(t - n_t, 0), h)),
        scratch_shapes=[pltpu.VMEM((HG_PAR, LANES, LANES), F32), pltpu.VMEM((s, w), F32)],
        compiler_params=_cparams(("parallel", "parallel", "arbitrary")),
        name="hgrn2",
    )(h3, h3, h3, h3, h3, lb_f.reshape(1, HG_W), lb_b.reshape(1, HG_W),
      norm_g.reshape(1, HG_W), tri_f, tri_r, lv_f, lv_r)


def _rope_tables(s):
    half = RET_DK // 2
    inv = ROPE_BASE ** (-jnp.arange(half, dtype=F32) / half)
    ang = jnp.arange(s, dtype=F32)[:, None] * inv[None, :]
    cos, sin = jnp.cos(ang), jnp.sin(ang)
    cos_t = jnp.concatenate([cos, cos, cos, cos], axis=-1)
    sin_t = jnp.concatenate([-sin, sin, -sin, sin], axis=-1)
    return cos_t, sin_t


def _rope_pair(x, cos_t, sin_t, lane):
    half = RET_DK // 2
    swapped = jnp.where(lane % RET_DK < half, pltpu.roll(x, LANES - half, 1), pltpu.roll(x, half, 1))
    return x * cos_t + swapped * sin_t


def _retention_kernel(lg_ref, q_ref, k_ref, v_ref, g_ref, cos_ref, sin_ref, ng_ref, o_ref,
                      sb_ref, sbc_ref, sf_ref, d_ref, *, t_rows, n_t):
    pair = pl.program_id(1)
    t = pl.program_id(2)
    tr = t_rows
    lane = lax.broadcasted_iota(jnp.int32, (tr, LANES), 1)
    row = lax.broadcasted_iota(jnp.int32, (tr, LANES), 0).astype(F32)
    cos_t = cos_ref[...]
    sin_t = sin_ref[...]
    k = _rope_pair(k_ref[...].astype(F32), cos_t, sin_t, lane) * (RET_DK ** -0.5)

    @pl.when(t == 0)
    def _():
        sbc_ref[...] = jnp.zeros_like(sbc_ref)

    @pl.when(t == n_t)
    def _():
        sf_ref[...] = jnp.zeros_like(sf_ref)
        ri = lax.broadcasted_iota(jnp.int32, (tr, tr), 0)
        ci = lax.broadcasted_iota(jnp.int32, (tr, tr), 1)
        rel = (ri - ci).astype(F32)
        for hh in range(2):
            lg_f = lg_ref[0, 2 * pair + hh]
            lg_b = lg_ref[1, 2 * pair + hh]
            d_ref[hh] = (jnp.exp(lg_f * jnp.maximum(rel, 0.0) + lg_b * jnp.maximum(-rel, 0.0))
                         + jnp.where(rel == 0.0, 1.0, 0.0))

    @pl.when(t < n_t)
    def _():
        blk = n_t - 1 - t
        for hh in range(2):
            lg_b = lg_ref[1, 2 * pair + hh]
            km = jnp.where(lane // RET_DK == hh, k, 0.0)
            vb = v_ref[:, hh * LANES:(hh + 1) * LANES].astype(BF16)
            cur = sbc_ref[hh]
            sb_ref[hh, blk] = cur
            zeta = jnp.exp(lg_b * row)
            sbc_ref[hh] = cur * jnp.exp(lg_b * tr) + _dot_tn((km * zeta).astype(BF16), vb)

    @pl.when(t >= n_t)
    def _():
        blk = t - n_t
        q = _rope_pair(q_ref[...].astype(F32), cos_t, sin_t, lane)
        qb = q.astype(BF16)
        for hh in range(2):
            lg_f = lg_ref[0, 2 * pair + hh]
            lg_b = lg_ref[1, 2 * pair + hh]
            km = jnp.where(lane // RET_DK == hh, k, 0.0)
            vb = v_ref[:, hh * LANES:(hh + 1) * LANES].astype(BF16)
            p = (_dot_nt(qb, km.astype(BF16)) * d_ref[hh]).astype(BF16)
            sf = sf_ref[hh]
            o = (_dot(p, vb)
                 + _dot((q * jnp.exp(lg_f * (row + 1.0))).astype(BF16), sf.astype(BF16))
                 + _dot((q * jnp.exp(lg_b * (tr - row))).astype(BF16), sb_ref[hh, blk].astype(BF16)))
            zeta = jnp.exp(lg_f * (tr - 1.0 - row))
            sf_ref[hh] = sf * jnp.exp(lg_f * tr) + _dot_tn((km * zeta).astype(BF16), vb)
            mu = jnp.mean(o, axis=-1, keepdims=True)
            oc = o - mu
            var = jnp.mean(oc * oc, axis=-1, keepdims=True)
            gate = g_ref[:, hh * LANES:(hh + 1) * LANES].astype(F32)
            y = oc * lax.rsqrt(var + EPS) * ng_ref[:, hh * LANES:(hh + 1) * LANES]
            o_ref[:, hh * LANES:(hh + 1) * LANES] = (y * (gate * jax.nn.sigmoid(gate))).astype(o_ref.dtype)


def retention_mixer(h3, norm_g, t_rows):
    bsz, s, _ = h3.shape
    n_t = s // t_rows
    cos_t, sin_t = _rope_tables(s)
    hidx = jnp.arange(RET_HEADS, dtype=F32)
    lg_f = jnp.log1p(-jnp.exp2(-5.0 - hidx))
    lg = jnp.stack([lg_f, lg_f[::-1]], axis=0)
    blk_of = lambda t: jnp.where(t < n_t, n_t - 1 - t, t - n_t)
    fwd_blk = lambda t: jnp.maximum(t - n_t, 0)
    return pl.pallas_call(
        functools.partial(_retention_kernel, t_rows=t_rows, n_t=n_t),
        out_shape=jax.ShapeDtypeStruct((bsz, s, RET_W), BF16),
        grid=(bsz, RET_HEADS // 2, 2 * n_t),
        in_specs=[pl.BlockSpec(memory_space=pltpu.SMEM),
                  pl.BlockSpec((None, t_rows, LANES), lambda b, p, t: (b, fwd_blk(t), CB_RET_Q + p)),
                  pl.BlockSpec((None, t_rows, LANES), lambda b, p, t: (b, blk_of(t), CB_RET_K + p)),
                  pl.BlockSpec((None, t_rows, 2 * LANES), lambda b, p, t: (b, blk_of(t), CB_RET_V // 2 + p)),
                  pl.BlockSpec((None, t_rows, 2 * LANES), lambda b, p, t: (b, fwd_blk(t), CB_RET_G // 2 + p)),
                  pl.BlockSpec((t_rows, LANES), lambda b, p, t: (blk_of(t), 0)),
                  pl.BlockSpec((t_rows, LANES), lambda b, p, t: (blk_of(t), 0)),
                  pl.BlockSpec((1, 2 * LANES), lambda b, p, t: (0, p))],
        out_specs=pl.BlockSpec((None, t_rows, 2 * LANES), lambda b, p, t: (b, fwd_blk(t), p)),
        scratch_shapes=[pltpu.VMEM((2, n_t, LANES, LANES), F32),
                        pltpu.VMEM((2, LANES, LANES), F32),
                        pltpu.VMEM((2, LANES, LANES), F32),
                        pltpu.VMEM((2, t_rows, t_rows), F32)],
        compiler_params=_cparams(("parallel", "parallel", "arbitrary")),
        name="retention",
    )(lg, h3, h3, h3, h3, cos_t, sin_t, norm_g.reshape(1, RET_W))


ATT_QB = 128


def _t5_bucket(rel):
    nb = REL_BUCKETS // 2
    max_exact = nb // 2
    sign_off = jnp.where(rel > 0, nb, 0)
    n = jnp.abs(rel)
    nf = jnp.maximum(n, 1).astype(F32)
    large = max_exact + (jnp.log(nf / max_exact) / math.log(REL_MAX_DIST / max_exact)
                         * (nb - max_exact)).astype(jnp.int32)
    large = jnp.minimum(large, nb - 1)
    return sign_off + jnp.where(n < max_exact, n, large)


def _dilated_kernel(tbl_ref, *refs, t_tok, s_len):
    n_g = len(DIL_GROUPS)
    grp = [refs[7 * g:7 * g + 7] for g in range(n_g)]
    qg_ref, kg_ref, bkt_ref, y_ref = refs[7 * n_g:7 * n_g + 4]
    og = refs[7 * n_g + 4:7 * n_g + 4 + n_g]
    lg = refs[7 * n_g + 4 + n_g:7 * n_g + 4 + 2 * n_g]
    bias_ref = refs[7 * n_g + 4 + 2 * n_g]
    slot = pl.program_id(1)
    t = pl.program_id(2)
    kw = ATT_QB + 2 * DIL_HALF

    @pl.when(t == 0)
    def _():
        for g in range(n_g):
            bkt = bkt_ref[g]
            bias = jnp.zeros((ATT_QB, kw), F32)
            for bucket in range(REL_BUCKETS):
                bias = jnp.where(bkt == bucket, tbl_ref[bucket, g * DIL_SLOTS + slot], bias)
            bias_ref[g] = bias

    ii = lax.broadcasted_iota(jnp.int32, (ATT_QB, kw), 0)
    jj = lax.broadcasted_iota(jnp.int32, (ATT_QB, kw), 1)
    in_band = jnp.abs(jj - DIL_HALF - ii) <= DIL_HALF
    qg = qg_ref[...] * (LANES ** -0.5)
    kg = kg_ref[...]

    for g, (_, dil) in enumerate(DIL_GROUPS):
        q_ref, kp_ref, k_ref, kn_ref, vp_ref, v_ref, vn_ref = grp[g]
        nq = t_tok // dil
        l_total = s_len // dil
        bias = bias_ref[g]

        def residue(r, dil=dil, nq=nq, l_total=l_total, bias=bias, g=g, q_ref=q_ref, kp_ref=kp_ref,
                    k_ref=k_ref, kn_ref=kn_ref, vp_ref=vp_ref, v_ref=v_ref, vn_ref=vn_ref):
            rows = lambda ref, n: ref[pl.ds(r, n, stride=dil), :]
            q = _rms_rows(rows(q_ref, nq), qg).astype(BF16)
            kall = jnp.concatenate([rows(kp_ref, DIL_HALF), rows(k_ref, nq), rows(kn_ref, DIL_HALF)], axis=0)
            kall = _rms_rows(kall, kg).astype(BF16)
            vall = jnp.concatenate([rows(vp_ref, DIL_HALF), rows(v_ref, nq), rows(vn_ref, DIL_HALF)],
                                   axis=0).astype(BF16)
            for qb in range(nq // ATT_QB):
                r0 = qb * ATT_QB
                key_idx = t * nq + r0 - DIL_HALF + jj
                valid = in_band & (key_idx >= 0) & (key_idx < l_total)
                s = _dot_nt(q[r0:r0 + ATT_QB], kall[r0:r0 + kw]) + bias
                s = jnp.where(valid, s, NEG_BIG)
                m = jnp.max(s, axis=-1, keepdims=True)
                p = jnp.exp(s - m)
                den = jnp.sum(p, axis=-1, keepdims=True)
                o = _dot(p.astype(BF16), vall[r0:r0 + kw]) / den
                dst = pl.ds(r + r0 * dil, ATT_QB, stride=dil)
                og[g][dst, :] = o
                lg[g][dst, :] = jnp.broadcast_to(m + jnp.log(den), (ATT_QB, LANES))

        if dil <= 4:
            for r in range(dil):
                residue(r)
        else:
            def body(r, carry, residue=residue):
                residue(r)
                return carry
            lax.fori_loop(0, dil, body, 0, unroll=4)

    l0, l1, l2 = lg[0][...], lg[1][...], lg[2][...]
    m = jnp.maximum(jnp.maximum(l0, l1), l2)
    w0, w1, w2 = jnp.exp(l0 - m), jnp.exp(l1 - m), jnp.exp(l2 - m)
    y = (w0 * og[0][...] + w1 * og[1][...] + w2 * og[2][...]) / (w0 + w1 + w2)
    y_ref[...] = y.astype(y_ref.dtype)


def dilated_mixer(hq_all, bsz, rel_bias, q_gain, k_gain, t_tok):
    n_g, m, _ = hq_all.shape
    s = m // bsz
    nb = s // t_tok
    cq, ck, cv = 0, DIL_SLOTS, 2 * DIL_SLOTS
    in_specs = [pl.BlockSpec(memory_space=pltpu.SMEM)]
    for g, (_, dil) in enumerate(DIL_GROUPS):
        halo = DIL_HALF * dil
        hpb = t_tok // halo
        n_halo = m // halo

        def main(cb, g=g):
            return pl.BlockSpec((None, t_tok, LANES), lambda b, sl, t: (g, b * nb + t, cb + sl))

        def prev(cb, g=g, halo=halo, hpb=hpb):
            return pl.BlockSpec((None, halo, LANES),
                                lambda b, sl, t: (g, jnp.maximum((b * nb + t) * hpb - 1, 0), cb + sl))

        def nxt(cb, g=g, halo=halo, hpb=hpb, n_halo=n_halo):
            return pl.BlockSpec((None, halo, LANES),
                                lambda b, sl, t: (g, jnp.minimum((b * nb + t + 1) * hpb, n_halo - 1), cb + sl))

        in_specs += [main(cq), prev(ck), main(ck), nxt(ck), prev(cv), main(cv), nxt(cv)]
    ii = jnp.arange(ATT_QB)[:, None]
    jj = jnp.arange(ATT_QB + 2 * DIL_HALF)[None, :]
    bkt = jnp.stack([_t5_bucket((jj - DIL_HALF - ii) * dil) for _, dil in DIL_GROUPS]).astype(jnp.int32)
    gain = pl.BlockSpec((1, LANES), lambda b, sl, t: (0, 0))
    in_specs += [gain, gain, pl.BlockSpec(bkt.shape, lambda b, sl, t: (0, 0, 0))]
    return pl.pallas_call(
        functools.partial(_dilated_kernel, t_tok=t_tok, s_len=s),
        out_shape=jax.ShapeDtypeStruct((m, DIL_W), BF16),
        grid=(bsz, DIL_SLOTS, nb),
        in_specs=in_specs,
        out_specs=pl.BlockSpec((t_tok, LANES), lambda b, sl, t: (b * nb + t, sl)),
        scratch_shapes=([pltpu.VMEM((t_tok, LANES), F32)] * (2 * n_g)
                        + [pltpu.VMEM((n_g, ATT_QB, ATT_QB + 2 * DIL_HALF), F32)]),
        compiler_params=_cparams(("parallel", "parallel", "arbitrary")),
        name="dilated_attn",
    )(rel_bias, *([hq_all] * (7 * n_g)), q_gain.reshape(1, LANES), k_gain.reshape(1, LANES), bkt)


def _tiles(bsz, s):
    m = bsz * s
    return dict(tm=min(1024, m), tm_small=min(512, m), hg_t=min(1024, s), ret_t=min(512, s),
                att_t=min(2048, s))


def kernel(x, w_in, w_out, w_up, w_down, norm_mix, norm_mlp, hg_lb_fwd, hg_lb_bwd,
           hg_norm, ret_norm, q_norm, k_norm, rel_bias):
    bsz, s, d = x.shape
    m = bsz * s
    tl = _tiles(bsz, s)
    lb_f = jnp.cumsum(jax.nn.softmax(hg_lb_fwd.astype(F32), axis=0), axis=0)
    lb_b = jnp.cumsum(jax.nn.softmax(hg_lb_bwd.astype(F32), axis=0), axis=0)
    lb_f = lb_f - lb_f[0]
    lb_b = lb_b - lb_b[0]
    w_out_b = w_out.astype(BF16)
    w_up_b, w_down_b = w_up.astype(BF16), w_down.astype(BF16)

    xf = x.reshape(m, d)
    xn = rmsnorm(xf, norm_mix[0], tl["tm_small"])
    mix_cols = CB_DIL * LANES
    for l in range(DEPTH):
        h3 = in_proj(xn, w_in, l, tl["tm"], 1536, 0, mix_cols).reshape(bsz, s, mix_cols)
        hq_all = in_proj_groups(xn, w_in, l, tl["tm"], mix_cols)
        ya = hgrn2_mixer(h3, lb_f[l], lb_b[l], hg_norm[l], tl["hg_t"]).reshape(m, HG_W)
        yb = retention_mixer(h3, ret_norm[l], tl["ret_t"]).reshape(m, RET_W)
        yc = dilated_mixer(hq_all, bsz, rel_bias, q_norm[l], k_norm[l], tl["att_t"])
        xf, hn = out_proj(ya, yb, yc, w_out_b, l, xf, norm_mlp[l], tl["tm_small"])
        last = l == DEPTH - 1
        xf, xn = mlp(hn, w_up_b, w_down_b, l, xf, norm_mix[(l + 1) % DEPTH],
                     tl["tm_small"], 1024, not last)
    return xf.reshape(bsz, s, d)
```
